```python
import math
import jax, jax.numpy as jnp
from jax import lax
import numpy as np

D_MODEL = 4096
BATCH = 1
SEQ = 8192
DEPTH = 2
DEC_BATCH = 4
DEC_SEQ = 4096
PAST_LEN = 128

HEAD_DIM = 128
GLA_HEADS = 8
GLA_DK = 128
GLA_DV = 256
GLA_RANK = 16
GLA_TAU = 16.0
GLA_CHUNK = 64
SWA_HQ = 16
SWA_HKV = 4
SWA_GROUP = SWA_HQ // SWA_HKV
WINDOW = 128
N_BUCKETS = 32
MAX_DISTANCE = 128
N_MEM = 256
CX_HEADS = 4
FFN_DIM = 11008
N_EXPERTS = 8
TOP_K = 2
EXPERT_DIM = 1792
EPS = 1e-6
NEG_INF = -1e30

GLA_QK_W = GLA_HEADS * GLA_DK
GLA_V_W = GLA_HEADS * GLA_DV
SWA_Q_W = SWA_HQ * HEAD_DIM
SWA_KV_W = SWA_HKV * HEAD_DIM
MIX_WIDTH = GLA_V_W + SWA_Q_W
IN_SPLITS = (GLA_QK_W, GLA_QK_W, GLA_V_W, GLA_V_W, GLA_RANK, GLA_RANK, SWA_Q_W, SWA_KV_W, SWA_KV_W)
IN_WIDTH = 9248
CX_W = CX_HEADS * HEAD_DIM
N_DENSE = (DEPTH + 1) // 2
N_MOE = DEPTH // 2

kernel_name = 'hybrid_gla_swa_bidir_encoder'


def rms_norm(x, g):
    xf = x.astype(jnp.float32)
    y = xf * lax.rsqrt(jnp.mean(xf * xf, axis=-1, keepdims=True) + EPS)
    return (y * g.astype(jnp.float32)).astype(x.dtype)


def t5_buckets(rel):
    half = N_BUCKETS // 2
    ret = (rel > 0).astype(np.int32) * half
    n = np.abs(rel)
    max_exact = half // 2
    large = max_exact + (np.log(np.maximum(n, 1) / max_exact) / np.log(MAX_DISTANCE / max_exact)
                         * (half - max_exact)).astype(np.int32)
    large = np.minimum(large, half - 1)
    return ret + np.where(n < max_exact, n, large)


def gla_direction(q, k, v, log_a, strict):
    b_sz, seq, heads, dk = q.shape
    dv = v.shape[-1]
    n_chunks = seq // GLA_CHUNK
    shp = (b_sz, n_chunks, GLA_CHUNK, heads)
    q = q.reshape(shp + (dk,))
    k = k.reshape(shp + (dk,))
    v = v.reshape(shp + (dv,))
    log_a = log_a.reshape(shp + (dk,))
    cum = jnp.cumsum(log_a, axis=2)
    last = cum[:, :, -1:]
    q_in = q * jnp.exp(cum)
    k_in = k * jnp.exp(-cum)
    k_end = k * jnp.exp(last - cum)
    mask = np.tril(np.ones((GLA_CHUNK, GLA_CHUNK), dtype=bool), -1 if strict else 0)
    scores = jnp.where(mask, jnp.einsum('bnchd,bnshd->bnhcs', q_in, k_in), 0.0)
    o_intra = jnp.einsum('bnhcs,bnshv->bnchv', scores, v)
    chunk_kv = jnp.einsum('bnchd,bnchv->bnhdv', k_end, v)
    chunk_decay = jnp.exp(last[:, :, 0])

    def step(state, inp):
        kv_n, decay_n = inp
        return decay_n[..., None] * state + kv_n, state

    s0 = jnp.zeros((b_sz, heads, dk, dv), q.dtype)
    _, s_prev = lax.scan(step, s0, (jnp.moveaxis(chunk_kv, 1, 0), jnp.moveaxis(chunk_decay, 1, 0)))
    s_prev = jnp.moveaxis(s_prev, 0, 1)
    o_inter = jnp.einsum('bnchd,bnhdv->bnchv', q_in, s_prev)
    return (o_intra + o_inter).reshape(b_sz, seq, heads, dv)


def gla_mixer(q, k, v, g, a_f, a_b, up_f, bias_f, up_b, bias_b, out_gain):
    b_sz, seq, _ = q.shape
    f32 = jnp.float32
    qh = q.astype(f32).reshape(b_sz, seq, GLA_HEADS, GLA_DK) * (GLA_DK ** -0.5)
    kh = k.astype(f32).reshape(b_sz, seq, GLA_HEADS, GLA_DK)
    vh = v.astype(f32).reshape(b_sz, seq, GLA_HEADS, GLA_DV)
    la_f = (jax.nn.log_sigmoid(a_f.astype(f32) @ up_f.astype(f32) + bias_f.astype(f32)) / GLA_TAU
            ).reshape(b_sz, seq, GLA_HEADS, GLA_DK)
    la_b = (jax.nn.log_sigmoid(a_b.astype(f32) @ up_b.astype(f32) + bias_b.astype(f32)) / GLA_TAU
            ).reshape(b_sz, seq, GLA_HEADS, GLA_DK)
    o_f = gla_direction(qh, kh, vh, la_f, False)
    o_b = gla_direction(qh[:, ::-1], kh[:, ::-1], vh[:, ::-1], la_b[:, ::-1], True)[:, ::-1]
    o = rms_norm(o_f + o_b, out_gain).reshape(b_sz, seq, GLA_V_W)
    return (o * jax.nn.silu(g.astype(f32))).astype(q.dtype)


def swa_mixer(q, k, v, q_gain, k_gain, sink, rel_bias):
    b_sz, seq, _ = q.shape
    nb = seq // WINDOW
    f32 = jnp.float32
    q = rms_norm(q.reshape(b_sz, seq, SWA_HQ, HEAD_DIM), q_gain)
    k = rms_norm(k.reshape(b_sz, seq, SWA_HKV, HEAD_DIM), k_gain)
    v = v.reshape(b_sz, seq, SWA_HKV, HEAD_DIM)
    q = q.reshape(b_sz, nb, WINDOW, SWA_HKV, SWA_GROUP, HEAD_DIM)

    def band(t):
        tp = jnp.pad(t, ((0, 0), (WINDOW, WINDOW), (0, 0), (0, 0)))
        tp = tp.reshape(b_sz, nb + 2, WINDOW, SWA_HKV, HEAD_DIM)
        return jnp.concatenate([tp[:, :-2], tp[:, 1:-1], tp[:, 2:]], axis=2)

    kw, vw = band(k), band(v)
    rel = np.arange(3 * WINDOW)[None, :] - WINDOW - np.arange(WINDOW)[:, None]
    bias = rel_bias[t5_buckets(rel)].astype(f32).transpose(2, 0, 1).reshape(
        SWA_HKV, SWA_GROUP, WINDOW, 3 * WINDOW)
    key_pos = np.arange(nb)[:, None] * WINDOW + np.arange(3 * WINDOW)[None, :] - WINDOW
    valid = (np.abs(rel) <= WINDOW)[None] & ((key_pos >= 0) & (key_pos < seq))[:, None, :]
    logits = jnp.einsum('bnqhgd,bnkhd->bnhgqk', q, kw, preferred_element_type=f32) * (HEAD_DIM ** -0.5) + bias
    logits = jnp.where(valid[None, :, None, None], logits, NEG_INF)
    sink_l = sink.astype(f32).reshape(1, 1, SWA_HKV, SWA_GROUP, 1, 1)
    m = jnp.maximum(jnp.max(logits, axis=-1, keepdims=True), sink_l)
    p = jnp.exp(logits - m)
    p = p / (jnp.sum(p, axis=-1, keepdims=True) + jnp.exp(sink_l - m))
    out = jnp.einsum('bnhgqk,bnkhd->bnqhgd', p.astype(vw.dtype), vw)
    return out.reshape(b_sz, seq, SWA_Q_W)


def memory_cross_attention(h, mem, g_x, g_m, wq, wkv, q_gain, k_gain, wo):
    b_sz, seq, _ = h.shape
    n_mem = mem.shape[1]
    q = (rms_norm(h, g_x) @ wq).reshape(b_sz, seq, CX_HEADS, HEAD_DIM)
    q = rms_norm(q, q_gain)
    k, v = jnp.split(rms_norm(mem, g_m) @ wkv, 2, axis=-1)
    k = rms_norm(k.reshape(b_sz, n_mem, CX_HEADS, HEAD_DIM), k_gain)
    v = v.reshape(b_sz, n_mem, CX_HEADS, HEAD_DIM)
    s = jnp.einsum('blhd,bmhd->bhlm', q, k, preferred_element_type=jnp.float32) * (HEAD_DIM ** -0.5)
    p = jax.nn.softmax(s, axis=-1)
    o = jnp.einsum('bhlm,bmhd->blhd', p.astype(v.dtype), v).reshape(b_sz, seq, CX_W)
    return o @ wo


def swiglu(x, w_gu, w_down):
    a, b = jnp.split(x @ w_gu, 2, axis=-1)
    return (jax.nn.silu(a) * b) @ w_down


def moe_swiglu(x, router, w_gu, w_down):
    b_sz, seq, d = x.shape
    xt = x.reshape(-1, d)
    logits = (xt @ router).astype(jnp.float32)
    top_v, top_i = lax.top_k(logits, TOP_K)
    top_w = jax.nn.softmax(top_v, axis=-1)
    gates = jnp.sum(jax.nn.one_hot(top_i, N_EXPERTS, dtype=jnp.float32) * top_w[..., None], axis=1)
    out = jnp.zeros(xt.shape, jnp.float32)
    for e in range(N_EXPERTS):
        out = out + gates[:, e:e + 1] * swiglu(xt, w_gu[e], w_down[e]).astype(jnp.float32)
    return out.astype(x.dtype).reshape(b_sz, seq, d)


def trunk(x, mem, rel_bias, norm_mix, w_in, gla_up_f, gla_bias_f, gla_up_b, gla_bias_b, gla_out_norm,
          swa_q_norm, swa_k_norm, swa_sink, w_out, norm_cross, norm_mem, cx_wq, cx_wkv, cx_q_norm,
          cx_k_norm, cx_wo, norm_ffn, ffn_w_gu, ffn_w_down, moe_router, moe_w_gu, moe_w_down):
    offsets = [int(o) for o in np.cumsum(IN_SPLITS)[:-1]]
    for i in range(DEPTH):
        hn = rms_norm(x, norm_mix[i])
        gq, gk, gv, gg, gaf, gab, sq, sk, sv = jnp.split(hn @ w_in[i], offsets, axis=-1)
        o_gla = gla_mixer(gq, gk, gv, gg, gaf, gab, gla_up_f[i], gla_bias_f[i], gla_up_b[i],
                          gla_bias_b[i], gla_out_norm[i])
        o_swa = swa_mixer(sq, sk, sv, swa_q_norm[i], swa_k_norm[i], swa_sink[i], rel_bias)
        x = x + jnp.concatenate([o_gla, o_swa.astype(o_gla.dtype)], axis=-1) @ w_out[i]
        x = x + memory_cross_attention(x, mem, norm_cross[i], norm_mem[i], cx_wq[i], cx_wkv[i],
                                       cx_q_norm[i], cx_k_norm[i], cx_wo[i])
        hn = rms_norm(x, norm_ffn[i])
        if i % 2 == 0:
            x = x + swiglu(hn, ffn_w_gu[i // 2], ffn_w_down[i // 2])
        else:
            x = x + moe_swiglu(hn, moe_router[i // 2], moe_w_gu[i // 2], moe_w_down[i // 2])
    return x


def setup_inputs(seed: int = 0) -> dict:
    key = jax.random.key(seed)
    ks = jax.random.split(key, 32)
    f32 = jnp.float32

    def nrm(k, shape, scale):
        return jax.random.normal(k, shape, f32) * scale

    def gain(k, shape):
        return 1.0 + 0.05 * jax.random.normal(k, shape, f32)

    return {
        'x_prompt': nrm(ks[0], (BATCH, SEQ, D_MODEL), 1.0),
        'x_sample': nrm(ks[1], (DEC_BATCH, DEC_SEQ, D_MODEL), 1.0),
        'mem_prompt': nrm(ks[2], (BATCH, N_MEM, D_MODEL), 1.0),
        'mem_sample': nrm(ks[3], (DEC_BATCH, N_MEM, D_MODEL), 1.0),
        'rel_bias': nrm(ks[4], (N_BUCKETS, SWA_HQ), 0.5),
        'norm_mix': gain(ks[5], (DEPTH, D_MODEL)),
        'w_in': nrm(ks[6], (DEPTH, D_MODEL, IN_WIDTH), D_MODEL ** -0.5),
        'gla_up_f': nrm(ks[7], (DEPTH, GLA_RANK, GLA_QK_W), GLA_RANK ** -0.5),
        'gla_bias_f': nrm(ks[8], (DEPTH, GLA_QK_W), 0.1),
        'gla_up_b': nrm(ks[9], (DEPTH, GLA_RANK, GLA_QK_W), GLA_RANK ** -0.5),
        'gla_bias_b': nrm(ks[10], (DEPTH, GLA_QK_W), 0.1),
        'gla_out_norm': gain(ks[11], (DEPTH, GLA_DV)),
        'swa_q_norm': gain(ks[12], (DEPTH, HEAD_DIM)),
        'swa_k_norm': gain(ks[13], (DEPTH, HEAD_DIM)),
        'swa_sink': nrm(ks[14], (DEPTH, SWA_HQ), 0.5),
        'w_out': nrm(ks[15], (DEPTH, MIX_WIDTH, D_MODEL), MIX_WIDTH ** -0.5),
        'norm_cross': gain(ks[16], (DEPTH, D_MODEL)),
        'norm_mem': gain(ks[17], (DEPTH, D_MODEL)),
        'cx_wq': nrm(ks[18], (DEPTH, D_MODEL, CX_W), D_MODEL ** -0.5),
        'cx_wkv': nrm(ks[19], (DEPTH, D_MODEL, 2 * CX_W), D_MODEL ** -0.5),
        'cx_q_norm': gain(ks[20], (DEPTH, HEAD_DIM)),
        'cx_k_norm': gain(ks[21], (DEPTH, HEAD_DIM)),
        'cx_wo': nrm(ks[22], (DEPTH, CX_W, D_MODEL), CX_W ** -0.5),
        'norm_ffn': gain(ks[23], (DEPTH, D_MODEL)),
        'ffn_w_gu': nrm(ks[24], (N_DENSE, D_MODEL, 2 * FFN_DIM), D_MODEL ** -0.5),
        'ffn_w_down': nrm(ks[25], (N_DENSE, FFN_DIM, D_MODEL), FFN_DIM ** -0.5),
        'moe_router': nrm(ks[26], (N_MOE, D_MODEL, N_EXPERTS), D_MODEL ** -0.5),
        'moe_w_gu': nrm(ks[27], (N_MOE, N_EXPERTS, D_MODEL, 2 * EXPERT_DIM), D_MODEL ** -0.5),
        'moe_w_down': nrm(ks[28], (N_MOE, N_EXPERTS, EXPERT_DIM, D_MODEL), EXPERT_DIM ** -0.5),
    }


def reference(x_prompt, x_sample, mem_prompt, mem_sample, rel_bias, norm_mix, w_in, gla_up_f, gla_bias_f,
              gla_up_b, gla_bias_b, gla_out_norm, swa_q_norm, swa_k_norm, swa_sink, w_out, norm_cross,
              norm_mem, cx_wq, cx_wkv, cx_q_norm, cx_k_norm, cx_wo, norm_ffn, ffn_w_gu, ffn_w_down,
              moe_router, moe_w_gu, moe_w_down):
    weights = (rel_bias, norm_mix, w_in, gla_up_f, gla_bias_f, gla_up_b, gla_bias_b, gla_out_norm,
               swa_q_norm, swa_k_norm, swa_sink, w_out, norm_cross, norm_mem, cx_wq, cx_wkv, cx_q_norm,
               cx_k_norm, cx_wo, norm_ffn, ffn_w_gu, ffn_w_down, moe_router, moe_w_gu, moe_w_down)
    y_prompt = trunk(x_prompt, mem_prompt, *weights)
    y_sample = trunk(x_sample, mem_sample, *weights)
    return (y_prompt, y_sample)
```

```python
import functools

import numpy as np
import jax
import jax.numpy as jnp
from jax import lax
from jax.experimental import pallas as pl
from jax.experimental.pallas import tpu as pltpu

F32 = jnp.float32
BF16 = jnp.bfloat16

HEAD_DIM = 128
GLA_HEADS = 8
GLA_DK = 128
GLA_DV = 256
GLA_RANK = 16
GLA_TAU = 16.0
GLA_CHUNK = 64
SWA_HQ = 16
SWA_HKV = 4
SWA_GROUP = SWA_HQ // SWA_HKV
WINDOW = 128
N_BUCKETS = 32
MAX_DISTANCE = 128
CX_HEADS = 4
TOP_K = 2
EPS = 1e-6
NEG_INF = -1e30

GLA_QK_W = GLA_HEADS * GLA_DK
GLA_V_W = GLA_HEADS * GLA_DV
SWA_Q_W = SWA_HQ * HEAD_DIM
SWA_KV_W = SWA_HKV * HEAD_DIM
CX_W = CX_HEADS * HEAD_DIM
OFF_GATE_END = 2 * GLA_QK_W + 2 * GLA_V_W
OFF_DECAY_END = OFF_GATE_END + 2 * GLA_RANK
MAIN_W = OFF_GATE_END + SWA_Q_W + 2 * SWA_KV_W

LANES = 128
VMEM_LIMIT_BYTES = 56 * 1024 * 1024


def _cparams(sem):
    return pltpu.CompilerParams(dimension_semantics=sem, vmem_limit_bytes=VMEM_LIMIT_BYTES)


def _rms_rows(x, gain):
    ms = jnp.mean(x * x, axis=-1, keepdims=True)
    return x * lax.rsqrt(ms + EPS) * gain


def _norm_block_to(x_ref, g_ref, hn_ref, slab=64):
    rows = x_ref.shape[0]
    slab = min(slab, rows)

    def body(r, carry):
        r0 = pl.multiple_of(r * slab, slab)
        hn_ref[pl.ds(r0, slab), :] = _rms_rows(x_ref[pl.ds(r0, slab), :], g_ref[...]).astype(hn_ref.dtype)
        return carry

    lax.fori_loop(0, rows // slab, body, 0)


def _pick(n, cands):
    for c in cands:
        if n % c == 0:
            return c
    return n


def _in_proj_kernel(x_ref, g_ref, w_ref, wa_ref, o_ref, oa_ref, hn_ref):
    @pl.when(pl.program_id(1) == 0)
    def _():
        _norm_block_to(x_ref, g_ref, hn_ref)
        oa_ref[...] = jnp.dot(hn_ref[...], wa_ref[...], preferred_element_type=F32)

    o_ref[...] = jnp.dot(hn_ref[...], w_ref[...], preferred_element_type=F32).astype(o_ref.dtype)


def in_proj(x, gain, w_main, w_decay):
    t, d = x.shape
    n = w_main.shape[1]
    bm = _pick(t, (512, 256, 128, 64, 32, 16, 8))
    bn = _pick(n, (1024, 768, 512, 256, 128))
    return pl.pallas_call(
        _in_proj_kernel,
        grid=(t // bm, n // bn),
        in_specs=[
            pl.BlockSpec((bm, d), lambda i, j: (i, 0)),
            pl.BlockSpec((1, d), lambda i, j: (0, 0)),
            pl.BlockSpec((d, bn), lambda i, j: (0, j)),
            pl.BlockSpec((d, LANES), lambda i, j: (0, 0)),
        ],
        out_specs=[
            pl.BlockSpec((bm, bn), lambda i, j: (i, j)),
            pl.BlockSpec((bm, LANES), lambda i, j: (i, 0)),
        ],
        out_shape=[jax.ShapeDtypeStruct((t, n), BF16), jax.ShapeDtypeStruct((t, LANES), F32)],
        scratch_shapes=[pltpu.VMEM((bm, d), BF16)],
        compiler_params=_cparams(("parallel", "arbitrary")),
        name="in_proj",
    )(x, gain.reshape(1, d), w_main, w_decay)


def _log_sigmoid(z):
    return jnp.minimum(z, 0.0) - jnp.log1p(jnp.exp(-jnp.abs(z)))


def _gla_kernel(*refs, reverse, finalize, rows):
    if finalize:
        q_ref, k_ref, v_ref, a_ref, u_ref, b_ref, of_ref, g_ref, gain_ref, o_ref, st_ref = refs
    else:
        q_ref, k_ref, v_ref, a_ref, u_ref, b_ref, o_ref, st_ref = refs
    c = GLA_CHUNK
    nch = rows // c

    @pl.when(pl.program_id(2) == 0)
    def _():
        st_ref[...] = jnp.zeros_like(st_ref)

    z = jnp.dot(a_ref[0].astype(BF16), u_ref[0], preferred_element_type=F32) + b_ref[0]
    la = _log_sigmoid(z) * (1.0 / GLA_TAU)

    ri = lax.broadcasted_iota(jnp.int32, (c, c), 0)
    ci = lax.broadcasted_iota(jnp.int32, (c, c), 1)
    if reverse:
        tri = (ci >= ri).astype(BF16)
        mask = ci > ri
    else:
        tri = (ci <= ri).astype(BF16)
        mask = ci <= ri
    scale = GLA_DK ** -0.5
    nt = (((1,), (1,)), ((), ()))
    tn = (((0,), (0,)), ((), ()))

    order = range(nch - 1, -1, -1) if reverse else range(nch)
    for ch in order:
        sl = slice(ch * c, (ch + 1) * c)
        la_c = la[sl]
        la_hi = la_c.astype(BF16)
        la_lo = (la_c - la_hi.astype(F32)).astype(BF16)
        cum = (jnp.dot(tri, la_hi, preferred_element_type=F32)
               + jnp.dot(tri, la_lo, preferred_element_type=F32))
        last = cum[0:1] if reverse else cum[c - 1:c]
        q = q_ref[0, sl, :].astype(F32)
        k = k_ref[0, sl, :].astype(F32)
        v = v_ref[0, sl, :]
        q_in = ((q * scale) * jnp.exp(cum)).astype(BF16)
        k_in = (k * jnp.exp(-cum)).astype(BF16)
        k_end = (k * jnp.exp(last - cum)).astype(BF16)
        sc = lax.dot_general(q_in, k_in, nt, preferred_element_type=F32)
        sc = jnp.where(mask, sc, 0.0).astype(BF16)
        st = st_ref[...]
        o = (jnp.dot(sc, v, preferred_element_type=F32)
             + lax.dot_general(q_in, st.astype(BF16), nt, preferred_element_type=F32))
        st_ref[...] = st * jnp.exp(last) + lax.dot_general(v, k_end, tn, preferred_element_type=F32)
        if finalize:
            o = o + of_ref[0, sl, :]
            o = _rms_rows(o, gain_ref[...])
            g = g_ref[0, sl, :].astype(F32)
            o_ref[0, sl, :] = (o * (g * jax.nn.sigmoid(g))).astype(o_ref.dtype)
        else:
            o_ref[0, sl, :] = o


def gla_direction(proj, decay_in, u, bias, *, reverse, o_fwd=None, out_gain=None):
    b, l, _ = proj.shape
    rows = _pick(l, (512, 256, 128, 64))
    nb = l // rows
    finalize = o_fwd is not None
    blk = (lambda s: nb - 1 - s) if reverse else (lambda s: s)
    kq = GLA_QK_W // GLA_DK
    kv = 2 * GLA_QK_W // GLA_DV
    kg = kv + GLA_HEADS
    in_specs = [
        pl.BlockSpec((1, rows, GLA_DK), lambda bi, h, s: (bi, blk(s), h)),
        pl.BlockSpec((1, rows, GLA_DK), lambda bi, h, s: (bi, blk(s), kq + h)),
        pl.BlockSpec((1, rows, GLA_DV), lambda bi, h, s: (bi, blk(s), kv + h)),
        pl.BlockSpec((1, rows, LANES), lambda bi, h, s: (bi, blk(s), 0)),
        pl.BlockSpec((1, LANES, GLA_DK), lambda bi, h, s: (h, 0, 0)),
        pl.BlockSpec((1, 1, GLA_DK), lambda bi, h, s: (h, 0, 0)),
    ]
    args = [proj, proj, proj, decay_in, u, bias]
    if finalize:
        in_specs += [
            pl.BlockSpec((1, rows, GLA_DV), lambda bi, h, s: (bi, blk(s), h)),
            pl.BlockSpec((1, rows, GLA_DV), lambda bi, h, s: (bi, blk(s), kg + h)),
            pl.BlockSpec((1, GLA_DV), lambda bi, h, s: (0, 0)),
        ]
        args += [o_fwd, proj, out_gain.reshape(1, GLA_DV)]
    out_dtype = BF16 if finalize else F32
    return pl.pallas_call(
        functools.partial(_gla_kernel, reverse=reverse, finalize=finalize, rows=rows),
        grid=(b, GLA_HEADS, nb),
        in_specs=in_specs,
        out_specs=pl.BlockSpec((1, rows, GLA_DV), lambda bi, h, s: (bi, blk(s), h)),
        out_shape=jax.ShapeDtypeStruct((b, l, GLA_V_W), out_dtype),
        scratch_shapes=[pltpu.VMEM((GLA_DV, GLA_DK), F32)],
        compiler_params=_cparams(("parallel", "parallel", "arbitrary")),
        name="gla_bwd" if reverse else "gla_fwd",
    )(*args)


def _t5_buckets(rel):
    half = N_BUCKETS // 2
    ret = (rel > 0).astype(np.int32) * half
    n = np.abs(rel)
    max_exact = half // 2
    large = max_exact + (np.log(np.maximum(n, 1) / max_exact) / np.log(MAX_DISTANCE / max_exact)
                         * (half - max_exact)).astype(np.int32)
    large = np.minimum(large, half - 1)
    return ret + np.where(n < max_exact, n, large)


def _bias_table_kernel(rb_ref, bkt_ref, o_ref):
    h = pl.program_id(0)
    bkt = bkt_ref[...]
    for g in range(SWA_GROUP):
        acc = jnp.zeros(bkt.shape, F32)
        for b in range(N_BUCKETS):
            acc = jnp.where(bkt == b, rb_ref[b, h * SWA_GROUP + g], acc)
        o_ref[0, g * WINDOW:(g + 1) * WINDOW, :] = acc


def swa_bias_table(rel_bias):
    rel = np.arange(3 * WINDOW)[None, :] - WINDOW - np.arange(WINDOW)[:, None]
    bkt = jnp.asarray(_t5_buckets(rel), jnp.int32)
    return pl.pallas_call(
        _bias_table_kernel,
        grid=(SWA_HKV,),
        in_specs=[
            pl.BlockSpec(memory_space=pltpu.SMEM),
            pl.BlockSpec((WINDOW, 3 * WINDOW), lambda h: (0, 0)),
        ],
        out_specs=pl.BlockSpec((1, SWA_GROUP * WINDOW, 3 * WINDOW), lambda h: (h, 0, 0)),
        out_shape=jax.ShapeDtypeStruct((SWA_HKV, SWA_GROUP * WINDOW, 3 * WINDOW), F32),
        compiler_params=_cparams(("parallel",)),
        name="swa_bias_table",
    )(rel_bias.astype(F32), bkt)


def _swa_kernel(sink_ref, q_ref, kp_ref, kc_ref, kn_ref, vp_ref, vc_ref, vn_ref, bias_ref,
                qg_ref, kg_ref, o_ref, *, nb):
    h = pl.program_id(1)
    i = pl.program_id(2)
    w = WINDOW
    q = q_ref[0]
    qs = jnp.concatenate([q[:, g * HEAD_DIM:(g + 1) * HEAD_DIM] for g in range(SWA_GROUP)], axis=0)
    qn = _rms_rows(qs.astype(F32), qg_ref[...]).astype(BF16)
    k = jnp.concatenate([kp_ref[0], kc_ref[0], kn_ref[0]], axis=0).astype(F32)
    kn = _rms_rows(k, kg_ref[...]).astype(BF16)
    v = jnp.concatenate([vp_ref[0], vc_ref[0], vn_ref[0]], axis=0)
    s = lax.dot_general(qn, kn, (((1,), (1,)), ((), ())), preferred_element_type=F32)
    logits = s * (HEAD_DIM ** -0.5) + bias_ref[0]
    row = lax.broadcasted_iota(jnp.int32, (w, 3 * w), 0)
    col = lax.broadcasted_iota(jnp.int32, (w, 3 * w), 1)
    rel = col - w - row
    valid = (jnp.abs(rel) <= w) & ((col >= w) | (i > 0)) & ((col < 2 * w) | (i < nb - 1))
    valid = jnp.concatenate([valid] * SWA_GROUP, axis=0)
    logits = jnp.where(valid, logits, NEG_INF)
    sink = jnp.concatenate(
        [jnp.full((w, 1), sink_ref[h * SWA_GROUP + g], F32) for g in range(SWA_GROUP)], axis=0)
    m = jnp.maximum(jnp.max(logits, axis=-1, keepdims=True), sink)
    p = jnp.exp(logits - m)
    denom = jnp.sum(p, axis=-1, keepdims=True) + jnp.exp(sink - m)
    o = jnp.dot(p.astype(BF16), v, preferred_element_type=F32) / denom
    o_ref[0] = jnp.concatenate([o[g * w:(g + 1) * w] for g in range(SWA_GROUP)], axis=1).astype(o_ref.dtype)


def swa_mixer(proj, bias_tab, sink, q_gain, k_gain):
    b, l, _ = proj.shape
    nb = l // WINDOW
    gw = SWA_GROUP * HEAD_DIM
    q0 = OFF_GATE_END // gw
    k0 = (OFF_GATE_END + SWA_Q_W) // HEAD_DIM
    v0 = k0 + SWA_HKV

    def kv_spec(base, shift):
        return pl.BlockSpec(
            (1, WINDOW, HEAD_DIM),
            lambda bi, h, i: (bi, jnp.clip(i + shift, 0, nb - 1), base + h))

    return pl.pallas_call(
        functools.partial(_swa_kernel, nb=nb),
        grid=(b, SWA_HKV, nb),
        in_specs=[
            pl.BlockSpec(memory_space=pltpu.SMEM),
            pl.BlockSpec((1, WINDOW, gw), lambda bi, h, i: (bi, i, q0 + h)),
            kv_spec(k0, -1), kv_spec(k0, 0), kv_spec(k0, 1),
            kv_spec(v0, -1), kv_spec(v0, 0), kv_spec(v0, 1),
            pl.BlockSpec((1, SWA_GROUP * WINDOW, 3 * WINDOW), lambda bi, h, i: (h, 0, 0)),
            pl.BlockSpec((1, HEAD_DIM), lambda bi, h, i: (0, 0)),
            pl.BlockSpec((1, HEAD_DIM), lambda bi, h, i: (0, 0)),
        ],
        out_specs=pl.BlockSpec((1, WINDOW, gw), lambda bi, h, i: (bi, i, h)),
        out_shape=jax.ShapeDtypeStruct((b, l, SWA_Q_W), BF16),
        compiler_params=_cparams(("parallel", "parallel", "arbitrary")),
        name="swa",
    )(sink.astype(F32), proj, proj, proj, proj, proj, proj, proj, bias_tab,
      q_gain.reshape(1, HEAD_DIM), k_gain.reshape(1, HEAD_DIM))


def _out_proj_kernel(a_ref, b_ref, wa_ref, wb_ref, x_ref, o_ref):
    o_ref[...] = (x_ref[...]
                  + jnp.dot(a_ref[...], wa_ref[...], preferred_element_type=F32)
                  + jnp.dot(b_ref[...], wb_ref[...], preferred_element_type=F32))


def out_proj(o_gla, o_swa, w_out, x):
    t, d = x.shape
    ka, kb = o_gla.shape[1], o_swa.shape[1]
    bm = _pick(t, (1024, 512, 256, 128, 64, 32, 16, 8))
    bn = _pick(d, (1024, 512, 256, 128))
    assert ka == kb and w_out.shape[0] == ka + kb
    return pl.pallas_call(
        _out_proj_kernel,
        grid=(t // bm, d // bn),
        in_specs=[
            pl.BlockSpec((bm, ka), lambda i, j: (i, 0)),
            pl.BlockSpec((bm, kb), lambda i, j: (i, 0)),
            pl.BlockSpec((ka, bn), lambda i, j: (0, j)),
            pl.BlockSpec((kb, bn), lambda i, j: (1, j)),
            pl.BlockSpec((bm, bn), lambda i, j: (i, j)),
        ],
        out_specs=pl.BlockSpec((bm, bn), lambda i, j: (i, j)),
        out_shape=jax.ShapeDtypeStruct((t, d), F32),
        compiler_params=_cparams(("parallel", "arbitrary")),
        name="out_proj",
    )(o_gla, o_swa, w_out, w_out, x)


def _cross_kv_kernel(m_ref, g_ref, w_ref, kg_ref, k_ref, v_ref, hn_ref):
    _norm_block_to(m_ref.at[0], g_ref, hn_ref)
    kv = jnp.dot(hn_ref[...], w_ref[...], preferred_element_type=F32)
    for h in range(CX_HEADS):
        kh = kv[:, h * HEAD_DIM:(h + 1) * HEAD_DIM]
        k_ref[0, :, h * HEAD_DIM:(h + 1) * HEAD_DIM] = _rms_rows(kh, kg_ref[...]).astype(k_ref.dtype)
    v_ref[0] = kv[:, CX_W:].astype(v_ref.dtype)


def cross_kv(mem, g_m, wkv, k_gain):
    b, n_mem, d = mem.shape
    return pl.pallas_call(
        _cross_kv_kernel,
        grid=(b,),
        in_specs=[
            pl.BlockSpec((1, n_mem, d), lambda i: (i, 0, 0)),
            pl.BlockSpec((1, d), lambda i: (0, 0)),
            pl.BlockSpec((d, 2 * CX_W), lambda i: (0, 0)),
            pl.BlockSpec((1, HEAD_DIM), lambda i: (0, 0)),
        ],
        out_specs=[
            pl.BlockSpec((1, n_mem, CX_W), lambda i: (i, 0, 0)),
            pl.BlockSpec((1, n_mem, CX_W), lambda i: (i, 0, 0)),
        ],
        out_shape=[jax.ShapeDtypeStruct((b, n_mem, CX_W), BF16)] * 2,
        scratch_shapes=[pltpu.VMEM((n_mem, d), BF16)],
        compiler_params=_cparams(("parallel",)),
        name="cross_kv",
    )(mem, g_m.reshape(1, d), wkv, k_gain.reshape(1, HEAD_DIM))


def _cross_kernel(x_ref, g_ref, wq_ref, qg_ref, k_ref, v_ref, wo_ref, o_ref, hn_ref):
    _norm_block_to(x_ref.at[0], g_ref, hn_ref)
    q = jnp.dot(hn_ref[...], wq_ref[...], preferred_element_type=F32)
    outs = []
    for h in range(CX_HEADS):
        hs = slice(h * HEAD_DIM, (h + 1) * HEAD_DIM)
        qh = _rms_rows(q[:, hs], qg_ref[...]).astype(BF16)
        s = lax.dot_general(qh, k_ref[0, :, hs], (((1,), (1,)), ((), ())),
                            preferred_element_type=F32) * (HEAD_DIM ** -0.5)
        m = jnp.max(s, axis=-1, keepdims=True)
        p = jnp.exp(s - m)
        denom = jnp.sum(p, axis=-1, keepdims=True)
        outs.append(jnp.dot(p.astype(BF16), v_ref[0, :, hs], preferred_element_type=F32) / denom)
    o = jnp.concatenate(outs, axis=1).astype(BF16)
    o_ref[0] = x_ref[0] + jnp.dot(o, wo_ref[...], preferred_element_type=F32)


def cross_attention(x, kmem, vmem, g_x, wq, q_gain, wo):
    b, l, d = x.shape
    n_mem = kmem.shape[1]
    bm = _pick(l, (512, 256, 128, 64, 32, 16, 8))
    return pl.pallas_call(
        _cross_kernel,
        grid=(b, l // bm),
        in_specs=[
            pl.BlockSpec((1, bm, d), lambda bi, i: (bi, i, 0)),
            pl.BlockSpec((1, d), lambda bi, i: (0, 0)),
            pl.BlockSpec((d, CX_W), lambda bi, i: (0, 0)),
            pl.BlockSpec((1, HEAD_DIM), lambda bi, i: (0, 0)),
            pl.BlockSpec((1, n_mem, CX_W), lambda bi, i: (bi, 0, 0)),
            pl.BlockSpec((1, n_mem, CX_W), lambda bi, i: (bi, 0, 0)),
            pl.BlockSpec((CX_W, d), lambda bi, i: (0, 0)),
        ],
        out_specs=pl.BlockSpec((1, bm, d), lambda bi, i: (bi, i, 0)),
        out_shape=jax.ShapeDtypeStruct((b, l, d), F32),
        scratch_shapes=[pltpu.VMEM((bm, d), BF16)],
        compiler_params=_cparams(("parallel", "arbitrary")),
        name="cross_attention",
    )(x, g_x.reshape(1, d), wq, q_gain.reshape(1, HEAD_DIM), kmem, vmem, wo)


def _router_kernel(x_ref, g_ref, rhi_ref, rlo_ref, gates_ref, *, n_experts):
    hn = _rms_rows(x_ref[...], g_ref[...])
    hi = hn.astype(BF16)
    lo = (hn - hi.astype(F32)).astype(BF16)
    logits = (jnp.dot(hi, rhi_ref[...], preferred_element_type=F32)
              + jnp.dot(hi, rlo_ref[...], preferred_element_type=F32)
              + jnp.dot(lo, rhi_ref[...], preferred_element_type=F32))
    lane = lax.broadcasted_iota(jnp.int32, logits.shape, 1).astype(F32)
    low = jnp.float32(-3.0e38)
    l1 = jnp.where(lane < n_experts, logits, low)
    m1 = jnp.max(l1, axis=-1, keepdims=True)
    i1 = jnp.min(jnp.where(l1 == m1, lane, float(LANES)), axis=-1, keepdims=True)
    l2 = jnp.where(lane == i1, low, l1)
    m2 = jnp.max(l2, axis=-1, keepdims=True)
    i2 = jnp.min(jnp.where(l2 == m2, lane, float(LANES)), axis=-1, keepdims=True)
    e2 = jnp.exp(m2 - m1)
    w1 = 1.0 / (1.0 + e2)
    w2 = e2 / (1.0 + e2)
    gates_ref[...] = jnp.where(lane == i1, w1, 0.0) + jnp.where(lane == i2, w2, 0.0)


def route_tokens(x, gain, router):
    t, d = x.shape
    n_experts = router.shape[1]
    r = jnp.zeros((d, LANES), F32).at[:, :n_experts].set(router.astype(F32))
    r_hi = r.astype(BF16)
    r_lo = (r - r_hi.astype(F32)).astype(BF16)
    bm = _pick(t, (256, 128, 64, 32, 16, 8))
    return pl.pallas_call(
        functools.partial(_router_kernel, n_experts=n_experts),
        grid=(t // bm,),
        in_specs=[
            pl.BlockSpec((bm, d), lambda i: (i, 0)),
            pl.BlockSpec((1, d), lambda i: (0, 0)),
            pl.BlockSpec((d, LANES), lambda i: (0, 0)),
            pl.BlockSpec((d, LANES), lambda i: (0, 0)),
        ],
        out_specs=pl.BlockSpec((bm, LANES), lambda i: (i, 0)),
        out_shape=jax.ShapeDtypeStruct((t, LANES), F32),
        compiler_params=_cparams(("parallel",)),
        name="moe_router",
    )(x, gain.reshape(1, d), r_hi, r_lo)


def _ffn_kernel(*refs, gated):
    if gated:
        x_ref, g_ref, gates_ref, wg_ref, wu_ref, wd_ref, o_ref, hn_ref = refs
    else:
        x_ref, g_ref, wg_ref, wu_ref, wd_ref, o_ref, hn_ref = refs
    e = pl.program_id(1)
    f = pl.program_id(2)

    @pl.when((e == 0) & (f == 0))
    def _():
        _norm_block_to(x_ref, g_ref, hn_ref)
        o_ref[...] = x_ref[...]

    hn = hn_ref[...]
    a = jnp.dot(hn, wg_ref[0], preferred_element_type=F32)
    b = jnp.dot(hn, wu_ref[0], preferred_element_type=F32)
    hid = (a * jax.nn.sigmoid(a)) * b
    if gated:
        gates = gates_ref[...]
        lane = lax.broadcasted_iota(jnp.int32, gates.shape, 1)
        hid = hid * jnp.sum(jnp.where(lane == e, gates, 0.0), axis=-1, keepdims=True)
    o_ref[...] += jnp.dot(hid.astype(BF16), wd_ref[0], preferred_element_type=F32)


def swiglu_ffn(x, gain, w_gu, w_down, gates=None):
    t, d = x.shape
    n_e, ffn = w_down.shape[0], w_down.shape[1]
    bm = _pick(t, (512, 256, 128, 64, 32, 16, 8))
    tf = _pick(ffn, (256, 128))
    nf = ffn // tf
    gated = gates is not None
    in_specs = [
        pl.BlockSpec((bm, d), lambda i, e, f: (i, 0)),
        pl.BlockSpec((1, d), lambda i, e, f: (0, 0)),
    ]
    args = [x, gain.reshape(1, d)]
    if gated:
        in_specs.append(pl.BlockSpec((bm, LANES), lambda i, e, f: (i, 0)))
        args.append(gates)
    in_specs += [
        pl.BlockSpec((1, d, tf), lambda i, e, f: (e, 0, f)),
        pl.BlockSpec((1, d, tf), lambda i, e, f: (e, 0, nf + f)),
        pl.BlockSpec((1, tf, d), lambda i, e, f: (e, f, 0)),
    ]
    args += [w_gu, w_gu, w_down]
    return pl.pallas_call(
        functools.partial(_ffn_kernel, gated=gated),
        grid=(t // bm, n_e, nf),
        in_specs=in_specs,
        out_specs=pl.BlockSpec((bm, d), lambda i, e, f: (i, 0)),
        out_shape=jax.ShapeDtypeStruct((t, d), F32),
        scratch_shapes=[pltpu.VMEM((bm, d), BF16)],
        compiler_params=_cparams(("parallel", "arbitrary", "arbitrary")),
        name="moe_ffn" if gated else "dense_ffn",
    )(*args)


def _prep_layer(i, p):
    w_in = p["w_in"][i]
    d = w_in.shape[0]
    w_main = jnp.concatenate([w_in[:, :OFF_GATE_END], w_in[:, OFF_DECAY_END:]], axis=1).astype(BF16)
    w_decay = jnp.zeros((d, LANES), BF16).at[:, :2 * GLA_RANK].set(
        w_in[:, OFF_GATE_END:OFF_DECAY_END].astype(BF16))

    def up_mat(up, row0):
        u = up.reshape(GLA_RANK, GLA_HEADS, GLA_DK).transpose(1, 0, 2).astype(BF16)
        return jnp.zeros((GLA_HEADS, LANES, GLA_DK), BF16).at[:, row0:row0 + GLA_RANK, :].set(u)

    return dict(
        norm_mix=p["norm_mix"][i], w_main=w_main, w_decay=w_decay,
        u_f=up_mat(p["gla_up_f"][i], 0), u_b=up_mat(p["gla_up_b"][i], GLA_RANK),
        bias_f=p["gla_bias_f"][i].reshape(GLA_HEADS, 1, GLA_DK).astype(F32),
        bias_b=p["gla_bias_b"][i].reshape(GLA_HEADS, 1, GLA_DK).astype(F32),
        gla_out_norm=p["gla_out_norm"][i], swa_q_norm=p["swa_q_norm"][i], swa_k_norm=p["swa_k_norm"][i],
        swa_sink=p["swa_sink"][i], w_out=p["w_out"][i].astype(BF16),
        norm_cross=p["norm_cross"][i], norm_mem=p["norm_mem"][i],
        cx_wq=p["cx_wq"][i].astype(BF16), cx_wkv=p["cx_wkv"][i].astype(BF16),
        cx_q_norm=p["cx_q_norm"][i], cx_k_norm=p["cx_k_norm"][i], cx_wo=p["cx_wo"][i].astype(BF16),
        norm_ffn=p["norm_ffn"][i],
    )


def _trunk(x, mem, layers, bias_tab):
    b, l, d = x.shape
    t = b * l
    for i, lw in enumerate(layers):
        proj, decay_in = in_proj(x.reshape(t, d), lw["norm_mix"], lw["w_main"], lw["w_decay"])
        proj = proj.reshape(b, l, MAIN_W)
        decay_in = decay_in.reshape(b, l, LANES)
        o_f = gla_direction(proj, decay_in, lw["u_f"], lw["bias_f"], reverse=False)
        o_gla = gla_direction(proj, decay_in, lw["u_b"], lw["bias_b"], reverse=True,
                              o_fwd=o_f, out_gain=lw["gla_out_norm"])
        o_swa = swa_mixer(proj, bias_tab, lw["swa_sink"], lw["swa_q_norm"], lw["swa_k_norm"])
        x1 = out_proj(o_gla.reshape(t, GLA_V_W), o_swa.reshape(t, SWA_Q_W), lw["w_out"], x.reshape(t, d))
        kmem, vmem = cross_kv(mem, lw["norm_mem"], lw["cx_wkv"], lw["cx_k_norm"])
        x2 = cross_attention(x1.reshape(b, l, d), kmem, vmem, lw["norm_cross"], lw["cx_wq"],
                             lw["cx_q_norm"], lw["cx_wo"]).reshape(t, d)
        if i % 2 == 0:
            x3 = swiglu_ffn(x2, lw["norm_ffn"], lw["ffn_w_gu"], lw["ffn_w_down"])
        else:
            gates = route_tokens(x2, lw["norm_ffn"], lw["moe_router"])
            x3 = swiglu_ffn(x2, lw["norm_ffn"], lw["moe_w_gu"], lw["moe_w_down"], gates=gates)
        x = x3.reshape(b, l, d)
    return x


def kernel(x_prompt, x_sample, mem_prompt, mem_sample, rel_bias, norm_mix, w_in, gla_up_f, gla_bias_f, gla_up_b, gla_bias_b, gla_out_norm, swa_q_norm, swa_k_norm, swa_sink, w_out, norm_cross, norm_mem, cx_wq, cx_wkv, cx_q_norm, cx_k_norm, cx_wo, norm_ffn, ffn_w_gu, ffn_w_down, moe_router, moe_w_gu, moe_w_down):
    p = dict(norm_mix=norm_mix, w_in=w_in, gla_up_f=gla_up_f, gla_bias_f=gla_bias_f, gla_up_b=gla_up_b,
             gla_bias_b=gla_bias_b, gla_out_norm=gla_out_norm, swa_q_norm=swa_q_norm, swa_k_norm=swa_k_norm,
             swa_sink=swa_sink, w_out=w_out, norm_cross=norm_cross, norm_mem=norm_mem, cx_wq=cx_wq,
             cx_wkv=cx_wkv, cx_q_norm=cx_q_norm, cx_k_norm=cx_k_norm, cx_wo=cx_wo, norm_ffn=norm_ffn)
    depth = w_in.shape[0]
    layers = []
    for i in range(depth):
        lw = _prep_layer(i, p)
        if i % 2 == 0:
            lw["ffn_w_gu"] = ffn_w_gu[i // 2][None].astype(BF16)
            lw["ffn_w_down"] = ffn_w_down[i // 2][None].astype(BF16)
        else:
            lw["moe_router"] = moe_router[i // 2]
            lw["moe_w_gu"] = moe_w_gu[i // 2].astype(BF16)
            lw["moe_w_down"] = moe_w_down[i // 2].astype(BF16)
        layers.append(lw)
    bias_tab = swa_bias_table(rel_bias)
    y_prompt = _trunk(x_prompt, mem_prompt, layers, bias_tab)
    y_sample = _trunk(x_sample, mem_sample, layers, bias_tab)
    return (y_prompt, y_sample)
```

```python
import functools

import numpy as np
import jax
import jax.numpy as jnp
from jax import lax
from jax.experimental import pallas as pl
from jax.experimental.pallas import tpu as pltpu

F32 = jnp.float32
BF16 = jnp.bfloat16

HEAD_DIM = 128
GLA_HEADS = 8
GLA_DK = 128
GLA_DV = 256
GLA_RANK = 16
GLA_TAU = 16.0
GLA_CHUNK = 64
SWA_HQ = 16
SWA_HKV = 4
SWA_GROUP = SWA_HQ // SWA_HKV
WINDOW = 128
N_BUCKETS = 32
MAX_DISTANCE = 128
CX_HEADS = 4
TOP_K = 2
EPS = 1e-6
NEG_INF = -1e30

GLA_QK_W = GLA_HEADS * GLA_DK
GLA_V_W = GLA_HEADS * GLA_DV
SWA_Q_W = SWA_HQ * HEAD_DIM
SWA_KV_W = SWA_HKV * HEAD_DIM
CX_W = CX_HEADS * HEAD_DIM
OFF_GATE_END = 2 * GLA_QK_W + 2 * GLA_V_W
OFF_DECAY_END = OFF_GATE_END + 2 * GLA_RANK
MAIN_W = OFF_GATE_END + SWA_Q_W + 2 * SWA_KV_W

LANES = 128
VMEM_LIMIT_BYTES = 56 * 1024 * 1024


def _cparams(sem):
    return pltpu.CompilerParams(dimension_semantics=sem, vmem_limit_bytes=VMEM_LIMIT_BYTES)


def _rms_rows(x, gain):
    ms = jnp.mean(x * x, axis=-1, keepdims=True)
    return x * lax.rsqrt(ms + EPS) * gain


def _norm_block_to(x_ref, g_ref, hn_ref, slab=64):
    rows = x_ref.shape[0]
    slab = min(slab, rows)

    def body(r, carry):
        r0 = pl.multiple_of(r * slab, slab)
        hn_ref[pl.ds(r0, slab), :] = _rms_rows(x_ref[pl.ds(r0, slab), :], g_ref[...]).astype(hn_ref.dtype)
        return carry

    lax.fori_loop(0, rows // slab, body, 0)


def _pick(n, cands):
    for c in cands:
        if n % c == 0:
            return c
    return n


def _in_proj_kernel(x_ref, g_ref, w_ref, wa_ref, o_ref, oa_ref, hn_ref):
    @pl.when(pl.program_id(1) == 0)
    def _():
        _norm_block_to(x_ref, g_ref, hn_ref)
        oa_ref[...] = jnp.dot(hn_ref[...], wa_ref[...], preferred_element_type=F32)

    o_ref[...] = jnp.dot(hn_ref[...], w_ref[...], preferred_element_type=F32).astype(o_ref.dtype)


def in_proj(x, gain, w_main, w_decay):
    t, d = x.shape
    n = w_main.shape[1]
    bm = _pick(t, (512, 256, 128, 64, 32, 16, 8))
    bn = _pick(n, (1024, 768, 512, 256, 128))
    return pl.pallas_call(
        _in_proj_kernel,
        grid=(t // bm, n // bn),
        in_specs=[
            pl.BlockSpec((bm, d), lambda i, j: (i, 0)),
            pl.BlockSpec((1, d), lambda i, j: (0, 0)),
            pl.BlockSpec((d, bn), lambda i, j: (0, j)),
            pl.BlockSpec((d, LANES), lambda i, j: (0, 0)),
        ],
        out_specs=[
            pl.BlockSpec((bm, bn), lambda i, j: (i, j)),
            pl.BlockSpec((bm, LANES), lambda i, j: (i, 0)),
        ],
        out_shape=[jax.ShapeDtypeStruct((t, n), BF16), jax.ShapeDtypeStruct((t, LANES), F32)],
        scratch_shapes=[pltpu.VMEM((bm, d), BF16)],
        compiler_params=_cparams(("parallel", "arbitrary")),
        name="in_proj",
    )(x, gain.reshape(1, d), w_main, w_decay)


def _log_sigmoid(z):
    return jnp.minimum(z, 0.0) - jnp.log1p(jnp.exp(-jnp.abs(z)))


def _gla_kernel(*refs, reverse, finalize, rows):
    if finalize:
        q_ref, k_ref, v_ref, a_ref, u_ref, b_ref, of_ref, g_ref, gain_ref, o_ref, st_ref = refs
    else:
        q_ref, k_ref, v_ref, a_ref, u_ref, b_ref, o_ref, st_ref = refs
    c = GLA_CHUNK
    nch = rows // c

    @pl.when(pl.program_id(2) == 0)
    def _():
        st_ref[...] = jnp.zeros_like(st_ref)

    z = jnp.dot(a_ref[0].astype(BF16), u_ref[0], preferred_element_type=F32) + b_ref[0]
    la = _log_sigmoid(z) * (1.0 / GLA_TAU)

    ri = lax.broadcasted_iota(jnp.int32, (c, c), 0)
    ci = lax.broadcasted_iota(jnp.int32, (c, c), 1)
    if reverse:
        tri = (ci >= ri).astype(BF16)
        mask = ci > ri
    else:
        tri = (ci <= ri).astype(BF16)
        mask = ci <= ri
    scale = GLA_DK ** -0.5
    nt = (((1,), (1,)), ((), ()))
    tn = (((0,), (0,)), ((), ()))

    parts = []
    for ch in range(nch):
        sl = slice(ch * c, (ch + 1) * c)
        la_c = la[sl]
        la_hi = la_c.astype(BF16)
        la_lo = (la_c - la_hi.astype(F32)).astype(BF16)
        cum = (jnp.dot(tri, la_hi, preferred_element_type=F32)
               + jnp.dot(tri, la_lo, preferred_element_type=F32))
        last = cum[0:1] if reverse else cum[c - 1:c]
        q = q_ref[0, sl, :].astype(F32)
        k = k_ref[0, sl, :].astype(F32)
        v = v_ref[0, sl, :]
        q_in = ((q * scale) * jnp.exp(cum)).astype(BF16)
        k_in = (k * jnp.exp(-cum)).astype(BF16)
        k_end = (k * jnp.exp(last - cum)).astype(BF16)
        sc = lax.dot_general(q_in, k_in, nt, preferred_element_type=F32)
        sc = jnp.where(mask, sc, 0.0).astype(BF16)
        o_intra = jnp.dot(sc, v, preferred_element_type=F32)
        kv = lax.dot_general(v, k_end, tn, preferred_element_type=F32)
        parts.append((q_in, o_intra, kv, jnp.exp(last)))

    st = st_ref[...]
    order = range(nch - 1, -1, -1) if reverse else range(nch)
    for ch in order:
        sl = slice(ch * c, (ch + 1) * c)
        q_in, o_intra, kv, decay = parts[ch]
        o = o_intra + lax.dot_general(q_in, st.astype(BF16), nt, preferred_element_type=F32)
        st = st * decay + kv
        if finalize:
            o = o + of_ref[0, sl, :]
            o = _rms_rows(o, gain_ref[...])
            g = g_ref[0, sl, :].astype(F32)
            o_ref[0, sl, :] = (o * (g * jax.nn.sigmoid(g))).astype(o_ref.dtype)
        else:
            o_ref[0, sl, :] = o
    st_ref[...] = st


def gla_direction(proj, decay_in, u, bias, *, reverse, o_fwd=None, out_gain=None):
    b, l, _ = proj.shape
    rows = _pick(l, (512, 256, 128, 64))
    nb = l // rows
    finalize = o_fwd is not None
    blk = (lambda s: nb - 1 - s) if reverse else (lambda s: s)
    kq = GLA_QK_W // GLA_DK
    kv = 2 * GLA_QK_W // GLA_DV
    kg = kv + GLA_HEADS
    in_specs = [
        pl.BlockSpec((1, rows, GLA_DK), lambda bi, h, s: (bi, blk(s), h)),
        pl.BlockSpec((1, rows, GLA_DK), lambda bi, h, s: (bi, blk(s), kq + h)),
        pl.BlockSpec((1, rows, GLA_DV), lambda bi, h, s: (bi, blk(s), kv + h)),
        pl.BlockSpec((1, rows, LANES), lambda bi, h, s: (bi, blk(s), 0)),
        pl.BlockSpec((1, LANES, GLA_DK), lambda bi, h, s: (h, 0, 0)),
        pl.BlockSpec((1, 1, GLA_DK), lambda bi, h, s: (h, 0, 0)),
    ]
    args = [proj, proj, proj, decay_in, u, bias]
    if finalize:
        in_specs += [
            pl.BlockSpec((1, rows, GLA_DV), lambda bi, h, s: (bi, blk(s), h)),
            pl.BlockSpec((1, rows, GLA_DV), lambda bi, h, s: (bi, blk(s), kg + h)),
            pl.BlockSpec((1, GLA_DV), lambda bi, h, s: (0, 0)),
        ]
        args += [o_fwd, proj, out_gain.reshape(1, GLA_DV)]
    out_dtype = BF16 if finalize else F32
    return pl.pallas_call(
        functools.partial(_gla_kernel, reverse=reverse, finalize=finalize, rows=rows),
        grid=(b, GLA_HEADS, nb),
        in_specs=in_specs,
        out_specs=pl.BlockSpec((1, rows, GLA_DV), lambda bi, h, s: (bi, blk(s), h)),
        out_shape=jax.ShapeDtypeStruct((b, l, GLA_V_W), out_dtype),
        scratch_shapes=[pltpu.VMEM((GLA_DV, GLA_DK), F32)],
        compiler_params=_cparams(("parallel", "parallel", "arbitrary")),
        name="gla_bwd" if reverse else "gla_fwd",
    )(*args)


def _t5_buckets(rel):
    half = N_BUCKETS // 2
    ret = (rel > 0).astype(np.int32) * half
    n = np.abs(rel)
    max_exact = half // 2
    large = max_exact + (np.log(np.maximum(n, 1) / max_exact) / np.log(MAX_DISTANCE / max_exact)
                         * (half - max_exact)).astype(np.int32)
    large = np.minimum(large, half - 1)
    return ret + np.where(n < max_exact, n, large)


def _bias_table_kernel(rb_ref, bkt_ref, o_ref):
    h = pl.program_id(0)
    bkt = bkt_ref[...]
    for g in range(SWA_GROUP):
        acc = jnp.zeros(bkt.shape, F32)
        for b in range(N_BUCKETS):
            acc = jnp.where(bkt == b, rb_ref[b, h * SWA_GROUP + g], acc)
        o_ref[0, g * WINDOW:(g + 1) * WINDOW, :] = acc


def swa_bias_table(rel_bias):
    rel = np.arange(3 * WINDOW)[None, :] - WINDOW - np.arange(WINDOW)[:, None]
    bkt = jnp.asarray(_t5_buckets(rel), jnp.int32)
    return pl.pallas_call(
        _bias_table_kernel,
        grid=(SWA_HKV,),
        in_specs=[
            pl.BlockSpec(memory_space=pltpu.SMEM),
            pl.BlockSpec((WINDOW, 3 * WINDOW), lambda h: (0, 0)),
        ],
        out_specs=pl.BlockSpec((1, SWA_GROUP * WINDOW, 3 * WINDOW), lambda h: (h, 0, 0)),
        out_shape=jax.ShapeDtypeStruct((SWA_HKV, SWA_GROUP * WINDOW, 3 * WINDOW), F32),
        compiler_params=_cparams(("parallel",)),
        name="swa_bias_table",
    )(rel_bias.astype(F32), bkt)


def _swa_kernel(sink_ref, q_ref, kp_ref, kc_ref, kn_ref, vp_ref, vc_ref, vn_ref, bias_ref,
                qg_ref, kg_ref, o_ref, *, nb):
    i = pl.program_id(1)
    w = WINDOW
    gw = SWA_GROUP * HEAD_DIM
    row = lax.broadcasted_iota(jnp.int32, (w, 3 * w), 0)
    col = lax.broadcasted_iota(jnp.int32, (w, 3 * w), 1)
    rel = col - w - row
    valid = (jnp.abs(rel) <= w) & ((col >= w) | (i > 0)) & ((col < 2 * w) | (i < nb - 1))
    valid = jnp.concatenate([valid] * SWA_GROUP, axis=0)
    for h in range(SWA_HKV):
        hs = slice(h * HEAD_DIM, (h + 1) * HEAD_DIM)
        q = q_ref[0, :, h * gw:(h + 1) * gw]
        qs = jnp.concatenate([q[:, g * HEAD_DIM:(g + 1) * HEAD_DIM] for g in range(SWA_GROUP)], axis=0)
        qn = _rms_rows(qs.astype(F32), qg_ref[...]).astype(BF16)
        k = jnp.concatenate([kp_ref[0, :, hs], kc_ref[0, :, hs], kn_ref[0, :, hs]], axis=0).astype(F32)
        kn = _rms_rows(k, kg_ref[...]).astype(BF16)
        v = jnp.concatenate([vp_ref[0, :, hs], vc_ref[0, :, hs], vn_ref[0, :, hs]], axis=0)
        s = lax.dot_general(qn, kn, (((1,), (1,)), ((), ())), preferred_element_type=F32)
        logits = s * (HEAD_DIM ** -0.5) + bias_ref[h]
        logits = jnp.where(valid, logits, NEG_INF)
        sink = jnp.concatenate(
            [jnp.full((w, 1), sink_ref[h * SWA_GROUP + g], F32) for g in range(SWA_GROUP)], axis=0)
        m = jnp.maximum(jnp.max(logits, axis=-1, keepdims=True), sink)
        p = jnp.exp(logits - m)
        denom = jnp.sum(p, axis=-1, keepdims=True) + jnp.exp(sink - m)
        o = jnp.dot(p.astype(BF16), v, preferred_element_type=F32) / denom
        o_ref[0, :, h * gw:(h + 1) * gw] = jnp.concatenate(
            [o[g * w:(g + 1) * w] for g in range(SWA_GROUP)], axis=1).astype(o_ref.dtype)


def swa_mixer(proj, bias_tab, sink, q_gain, k_gain):
    b, l, _ = proj.shape
    nb = l // WINDOW
    q0 = OFF_GATE_END // SWA_Q_W
    k0 = (OFF_GATE_END + SWA_Q_W) // SWA_KV_W
    v0 = k0 + 1
    assert OFF_GATE_END % SWA_Q_W == 0 and (OFF_GATE_END + SWA_Q_W) % SWA_KV_W == 0

    def kv_spec(base, shift):
        return pl.BlockSpec(
            (1, WINDOW, SWA_KV_W),
            lambda bi, i: (bi, jnp.clip(i + shift, 0, nb - 1), base))

    return pl.pallas_call(
        functools.partial(_swa_kernel, nb=nb),
        grid=(b, nb),
        in_specs=[
            pl.BlockSpec(memory_space=pltpu.SMEM),
            pl.BlockSpec((1, WINDOW, SWA_Q_W), lambda bi, i: (bi, i, q0)),
            kv_spec(k0, -1), kv_spec(k0, 0), kv_spec(k0, 1),
            kv_spec(v0, -1), kv_spec(v0, 0), kv_spec(v0, 1),
            pl.BlockSpec((SWA_HKV, SWA_GROUP * WINDOW, 3 * WINDOW), lambda bi, i: (0, 0, 0)),
            pl.BlockSpec((1, HEAD_DIM), lambda bi, i: (0, 0)),
            pl.BlockSpec((1, HEAD_DIM), lambda bi, i: (0, 0)),
        ],
        out_specs=pl.BlockSpec((1, WINDOW, SWA_Q_W), lambda bi, i: (bi, i, 0)),
        out_shape=jax.ShapeDtypeStruct((b, l, SWA_Q_W), BF16),
        compiler_params=_cparams(("parallel", "arbitrary")),
        name="swa",
    )(sink.astype(F32), proj, proj, proj, proj, proj, proj, proj, bias_tab,
      q_gain.reshape(1, HEAD_DIM), k_gain.reshape(1, HEAD_DIM))


def _out_proj_kernel(a_ref, b_ref, wa_ref, wb_ref, x_ref, o_ref):
    o_ref[...] = (x_ref[...]
                  + jnp.dot(a_ref[...], wa_ref[...], preferred_element_type=F32)
                  + jnp.dot(b_ref[...], wb_ref[...], preferred_element_type=F32))


def out_proj(o_gla, o_swa, w_out, x):
    t, d = x.shape
    ka, kb = o_gla.shape[1], o_swa.shape[1]
    bm = _pick(t, (1024, 512, 256, 128, 64, 32, 16, 8))
    bn = _pick(d, (1024, 512, 256, 128))
    assert ka == kb and w_out.shape[0] == ka + kb
    return pl.pallas_call(
        _out_proj_kernel,
        grid=(t // bm, d // bn),
        in_specs=[
            pl.BlockSpec((bm, ka), lambda i, j: (i, 0)),
            pl.BlockSpec((bm, kb), lambda i, j: (i, 0)),
            pl.BlockSpec((ka, bn), lambda i, j: (0, j)),
            pl.BlockSpec((kb, bn), lambda i, j: (1, j)),
            pl.BlockSpec((bm, bn), lambda i, j: (i, j)),
        ],
        out_specs=pl.BlockSpec((bm, bn), lambda i, j: (i, j)),
        out_shape=jax.ShapeDtypeStruct((t, d), F32),
        compiler_params=_cparams(("parallel", "arbitrary")),
        name="out_proj",
    )(o_gla, o_swa, w_out, w_out, x)


def _cross_kv_kernel(m_ref, g_ref, w_ref, kg_ref, k_ref, v_ref, hn_ref):
    _norm_block_to(m_ref.at[0], g_ref, hn_ref)
    kv = jnp.dot(hn_ref[...], w_ref[...], preferred_element_type=F32)
    for h in range(CX_HEADS):
        kh = kv[:, h * HEAD_DIM:(h + 1) * HEAD_DIM]
        k_ref[0, :, h * HEAD_DIM:(h + 1) * HEAD_DIM] = _rms_rows(kh, kg_ref[...]).astype(k_ref.dtype)
    v_ref[0] = kv[:, CX_W:].astype(v_ref.dtype)


def cross_kv(mem, g_m, wkv, k_gain):
    b, n_mem, d = mem.shape
    return pl.pallas_call(
        _cross_kv_kernel,
        grid=(b,),
        in_specs=[
            pl.BlockSpec((1, n_mem, d), lambda i: (i, 0, 0)),
            pl.BlockSpec((1, d), lambda i: (0, 0)),
            pl.BlockSpec((d, 2 * CX_W), lambda i: (0, 0)),
            pl.BlockSpec((1, HEAD_DIM), lambda i: (0, 0)),
        ],
        out_specs=[
            pl.BlockSpec((1, n_mem, CX_W), lambda i: (i, 0, 0)),
            pl.BlockSpec((1, n_mem, CX_W), lambda i: (i, 0, 0)),
        ],
        out_shape=[jax.ShapeDtypeStruct((b, n_mem, CX_W), BF16)] * 2,
        scratch_shapes=[pltpu.VMEM((n_mem, d), BF16)],
        compiler_params=_cparams(("parallel",)),
        name="cross_kv",
    )(mem, g_m.reshape(1, d), wkv, k_gain.reshape(1, HEAD_DIM))


def _cross_kernel(x_ref, g_ref, wq_ref, qg_ref, k_ref, v_ref, wo_ref, o_ref, hn_ref):
    _norm_block_to(x_ref.at[0], g_ref, hn_ref)
    q = jnp.dot(hn_ref[...], wq_ref[...], preferred_element_type=F32)
    outs = []
    for h in range(CX_HEADS):
        hs = slice(h * HEAD_DIM, (h + 1) * HEAD_DIM)
        qh = _rms_rows(q[:, hs], qg_ref[...]).astype(BF16)
        s = lax.dot_general(qh, k_ref[0, :, hs], (((1,), (1,)), ((), ())),
                            preferred_element_type=F32) * (HEAD_DIM ** -0.5)
        m = jnp.max(s, axis=-1, keepdims=True)
        p = jnp.exp(s - m)
        denom = jnp.sum(p, axis=-1, keepdims=True)
        outs.append(jnp.dot(p.astype(BF16), v_ref[0, :, hs], preferred_element_type=F32) / denom)
    o = jnp.concatenate(outs, axis=1).astype(BF16)
    o_ref[0] = x_ref[0] + jnp.dot(o, wo_ref[...], preferred_element_type=F32)


def cross_attention(x, kmem, vmem, g_x, wq, q_gain, wo):
    b, l, d = x.shape
    n_mem = kmem.shape[1]
    bm = _pick(l, (512, 256, 128, 64, 32, 16, 8))
    return pl.pallas_call(
        _cross_kernel,
        grid=(b, l // bm),
        in_specs=[
            pl.BlockSpec((1, bm, d), lambda bi, i: (bi, i, 0)),
            pl.BlockSpec((1, d), lambda bi, i: (0, 0)),
            pl.BlockSpec((d, CX_W), lambda bi, i: (0, 0)),
            pl.BlockSpec((1, HEAD_DIM), lambda bi, i: (0, 0)),
            pl.BlockSpec((1, n_mem, CX_W), lambda bi, i: (bi, 0, 0)),
            pl.BlockSpec((1, n_mem, CX_W), lambda bi, i: (bi, 0, 0)),
            pl.BlockSpec((CX_W, d), lambda bi, i: (0, 0)),
        ],
        out_specs=pl.BlockSpec((1, bm, d), lambda bi, i: (bi, i, 0)),
        out_shape=jax.ShapeDtypeStruct((b, l, d), F32),
        scratch_shapes=[pltpu.VMEM((bm, d), BF16)],
        compiler_params=_cparams(("parallel", "arbitrary")),
        name="cross_attention",
    )(x, g_x.reshape(1, d), wq, q_gain.reshape(1, HEAD_DIM), kmem, vmem, wo)


def _router_kernel(x_ref, g_ref, rhi_ref, rlo_ref, idx_ref, w_ref, *, n_experts):
    hn = _rms_rows(x_ref[...], g_ref[...])
    hi = hn.astype(BF16)
    lo = (hn - hi.astype(F32)).astype(BF16)
    logits = (jnp.dot(hi, rhi_ref[...], preferred_element_type=F32)
              + jnp.dot(hi, rlo_ref[...], preferred_element_type=F32)
              + jnp.dot(lo, rhi_ref[...], preferred_element_type=F32))
    lane = lax.broadcasted_iota(jnp.int32, logits.shape, 1).astype(F32)
    low = jnp.float32(-3.0e38)
    l1 = jnp.where(lane < n_experts, logits, low)
    m1 = jnp.max(l1, axis=-1, keepdims=True)
    i1 = jnp.min(jnp.where(l1 == m1, lane, float(LANES)), axis=-1, keepdims=True)
    l2 = jnp.where(lane == i1, low, l1)
    m2 = jnp.max(l2, axis=-1, keepdims=True)
    i2 = jnp.min(jnp.where(l2 == m2, lane, float(LANES)), axis=-1, keepdims=True)
    e2 = jnp.exp(m2 - m1)
    w1 = 1.0 / (1.0 + e2)
    w2 = e2 / (1.0 + e2)
    idx_ref[...] = jnp.where(lane == 0.0, i1, jnp.where(lane == 1.0, i2, 0.0)).astype(jnp.int32)
    w_ref[...] = jnp.where(lane == 0.0, w1, jnp.where(lane == 1.0, w2, 0.0))


def route_tokens(x, gain, router):
    t, d = x.shape
    n_experts = router.shape[1]
    r = jnp.zeros((d, LANES), F32).at[:, :n_experts].set(router.astype(F32))
    r_hi = r.astype(BF16)
    r_lo = (r - r_hi.astype(F32)).astype(BF16)
    bm = _pick(t, (256, 128, 64, 32, 16, 8))
    return pl.pallas_call(
        functools.partial(_router_kernel, n_experts=n_experts),
        grid=(t // bm,),
        in_specs=[
            pl.BlockSpec((bm, d), lambda i: (i, 0)),
            pl.BlockSpec((1, d), lambda i: (0, 0)),
            pl.BlockSpec((d, LANES), lambda i: (0, 0)),
            pl.BlockSpec((d, LANES), lambda i: (0, 0)),
        ],
        out_specs=[pl.BlockSpec((bm, LANES), lambda i: (i, 0)), pl.BlockSpec((bm, LANES), lambda i: (i, 0))],
        out_shape=[jax.ShapeDtypeStruct((t, LANES), jnp.int32), jax.ShapeDtypeStruct((t, LANES), F32)],
        compiler_params=_cparams(("parallel",)),
        name="moe_router",
    )(x, gain.reshape(1, d), r_hi, r_lo)


def _swiglu_step(hn, wg_ref, wu_ref, wd_ref):
    a = jnp.dot(hn, wg_ref[0], preferred_element_type=F32)
    b = jnp.dot(hn, wu_ref[0], preferred_element_type=F32)
    hid = (a * jax.nn.sigmoid(a)) * b
    return jnp.dot(hid.astype(BF16), wd_ref[0], preferred_element_type=F32)


def _ffn_kernel(x_ref, g_ref, wg_ref, wu_ref, wd_ref, o_ref, hn_ref):
    @pl.when(pl.program_id(1) == 0)
    def _():
        _norm_block_to(x_ref, g_ref, hn_ref)
        o_ref[...] = x_ref[...]

    o_ref[...] += _swiglu_step(hn_ref[...], wg_ref, wu_ref, wd_ref)


def swiglu_ffn(x, gain, w_gu, w_down):
    t, d = x.shape
    ffn = w_down.shape[1]
    bm = _pick(t, (512, 256, 128, 64, 32, 16, 8))
    tf = _pick(ffn, (256, 128))
    nf = ffn // tf
    return pl.pallas_call(
        _ffn_kernel,
        grid=(t // bm, nf),
        in_specs=[
            pl.BlockSpec((bm, d), lambda i, f: (i, 0)),
            pl.BlockSpec((1, d), lambda i, f: (0, 0)),
            pl.BlockSpec((1, d, tf), lambda i, f: (0, 0, f)),
            pl.BlockSpec((1, d, tf), lambda i, f: (0, 0, nf + f)),
            pl.BlockSpec((1, tf, d), lambda i, f: (0, f, 0)),
        ],
        out_specs=pl.BlockSpec((bm, d), lambda i, f: (i, 0)),
        out_shape=jax.ShapeDtypeStruct((t, d), F32),
        scratch_shapes=[pltpu.VMEM((bm, d), BF16)],
        compiler_params=_cparams(("parallel", "arbitrary")),
        name="dense_ffn",
    )(x, gain.reshape(1, d), w_gu, w_gu, w_down)


MOE_TILE_ROWS = 512
MOE_MOVE_TOKENS = 128
MOE_ZERO_ROWS = 128


def _moe_plan(top_idx, n_e, bm, n_tiles):
    e_flat = top_idx[:, :TOP_K].reshape(-1)
    onehot = (e_flat[:, None] == jnp.arange(n_e, dtype=jnp.int32)[None, :]).astype(jnp.int32)
    csum = jnp.cumsum(onehot, axis=0)
    cnt = csum[-1]
    rank = jnp.sum(onehot * (csum - 1), axis=1)
    padded = ((cnt + bm - 1) // bm) * bm
    gend = jnp.cumsum(padded)
    gstart = gend - padded
    dest = (jnp.sum(onehot * gstart[None, :], axis=1) + rank).astype(jnp.int32)
    n_valid = (gend[-1] // bm).astype(jnp.int32)
    tile_start = jnp.arange(n_tiles, dtype=jnp.int32) * bm
    tile_e = jnp.sum((tile_start[:, None] >= gend[None, :]).astype(jnp.int32), axis=1)
    tile_e = jnp.minimum(tile_e, n_e - 1)
    last_e = jnp.sum(jnp.where(jnp.arange(n_tiles) == n_valid - 1, tile_e, 0))
    tile_e = jnp.where(jnp.arange(n_tiles) < n_valid, tile_e, last_e).astype(jnp.int32)
    return dest, tile_e, n_valid.reshape(1), gend.astype(jnp.int32), cnt.astype(jnp.int32)


def _dispatch_kernel(gend_ref, cnt_ref, nv_ref, dest_ref, x_hbm, xs_hbm, zbuf, sems, zsem, *,
                     bt, bm, n_e, n_steps, n_tiles):
    i = pl.program_id(0)

    @pl.when(i == 0)
    def _():
        zbuf[...] = jnp.zeros_like(zbuf)
        zr = zbuf.shape[0]

        def clear_tile(base):
            copies = [pltpu.make_async_copy(
                zbuf, xs_hbm.at[pl.ds(pl.multiple_of(base + z * zr, zr), zr)], zsem)
                for z in range(bm // zr)]
            for cp in copies:
                cp.start()
            for cp in copies:
                cp.wait()

        for e in range(n_e):
            @pl.when(cnt_ref[e] > 0)
            def _():
                clear_tile(gend_ref[e] - bm)

        for back in range(1, min(n_e, n_tiles) + 1):
            @pl.when(n_tiles - back >= nv_ref[0])
            def _():
                clear_tile((n_tiles - back) * bm)

    slot = i % 2

    def issue(r, carry):
        tok = i * bt + r
        for k in range(TOP_K):
            pltpu.make_async_copy(x_hbm.at[pl.ds(tok, 1)],
                                  xs_hbm.at[pl.ds(dest_ref[0, 0, TOP_K * r + k], 1)],
                                  sems.at[slot]).start()
        return carry

    lax.fori_loop(0, bt, issue, 0)

    def wait_step(s):
        pltpu.make_async_copy(xs_hbm.at[pl.ds(0, TOP_K * bt)], xs_hbm.at[pl.ds(0, TOP_K * bt)],
                              sems.at[s]).wait()

    @pl.when(i > 0)
    def _():
        wait_step(1 - slot)

    @pl.when(i == n_steps - 1)
    def _():
        wait_step(slot)


def moe_dispatch(x, dest, gend, cnt, n_valid, n_tiles, bm):
    t, d = x.shape
    n_rows = n_tiles * bm
    bt = _pick(t, (MOE_MOVE_TOKENS, 64, 32, 16, 8))
    n_steps = t // bt
    zr = min(MOE_ZERO_ROWS, bm)
    return pl.pallas_call(
        functools.partial(_dispatch_kernel, bt=bt, bm=bm, n_e=cnt.shape[0], n_steps=n_steps,
                          n_tiles=n_tiles),
        grid_spec=pltpu.PrefetchScalarGridSpec(
            num_scalar_prefetch=3,
            grid=(n_steps,),
            in_specs=[
                pl.BlockSpec((1, 1, TOP_K * bt), lambda i, ge, cn, nv: (i, 0, 0), memory_space=pltpu.SMEM),
                pl.BlockSpec(memory_space=pl.ANY),
            ],
            out_specs=pl.BlockSpec(memory_space=pl.ANY),
            scratch_shapes=[pltpu.VMEM((zr, d), F32), pltpu.SemaphoreType.DMA((2,)),
                            pltpu.SemaphoreType.DMA(())],
        ),
        out_shape=jax.ShapeDtypeStruct((n_rows, d), F32),
        compiler_params=_cparams(("arbitrary",)),
        name="moe_dispatch",
    )(gend, cnt, n_valid, dest.reshape(n_steps, 1, TOP_K * bt), x)


def _moe_ffn_kernel(te_ref, nv_ref, x_ref, g_ref, wg_ref, wu_ref, wd_ref, o_ref, hn_ref, *, nf):
    j = pl.program_id(0)
    f = pl.program_id(1)
    valid = j < nv_ref[0]

    @pl.when(valid & (f == 0))
    def _():
        _norm_block_to(x_ref, g_ref, hn_ref)
        o_ref[...] = _swiglu_step(hn_ref[...], wg_ref, wu_ref, wd_ref)

    @pl.when(valid & (f > 0))
    def _():
        o_ref[...] += _swiglu_step(hn_ref[...], wg_ref, wu_ref, wd_ref)

    @pl.when(jnp.logical_not(valid) & (f == nf - 1))
    def _():
        o_ref[...] = jnp.zeros_like(o_ref)


def moe_expert_ffn(xs, gain, w_gu, w_down, tile_e, n_valid, bm):
    n_rows, d = xs.shape
    n_tiles = n_rows // bm
    ffn = w_down.shape[1]
    tf = _pick(ffn, (256, 128))
    nf = ffn // tf

    def fidx(j, f, nv):
        return jnp.where(j < nv[0], f, nf - 1)

    return pl.pallas_call(
        functools.partial(_moe_ffn_kernel, nf=nf),
        grid_spec=pltpu.PrefetchScalarGridSpec(
            num_scalar_prefetch=2,
            grid=(n_tiles, nf),
            in_specs=[
                pl.BlockSpec((bm, d), lambda j, f, te, nv: (jnp.minimum(j, nv[0] - 1), 0)),
                pl.BlockSpec((1, d), lambda j, f, te, nv: (0, 0)),
                pl.BlockSpec((1, d, tf), lambda j, f, te, nv: (te[j], 0, fidx(j, f, nv))),
                pl.BlockSpec((1, d, tf), lambda j, f, te, nv: (te[j], 0, nf + fidx(j, f, nv))),
                pl.BlockSpec((1, tf, d), lambda j, f, te, nv: (te[j], fidx(j, f, nv), 0)),
            ],
            out_specs=pl.BlockSpec((bm, d), lambda j, f, te, nv: (j, 0)),
            scratch_shapes=[pltpu.VMEM((bm, d), BF16)],
        ),
        out_shape=jax.ShapeDtypeStruct((n_rows, d), F32),
        compiler_params=_cparams(("arbitrary", "arbitrary")),
        name="moe_ffn",
    )(tile_e, n_valid, xs, gain.reshape(1, d), w_gu, w_gu, w_down)


def _combine_kernel(pos_ref, posn_ref, x_ref, w_ref, y_hbm, o_ref, ybuf, sems, *, bt, n_steps):
    i = pl.program_id(0)
    slot = i % 2

    def issue(p_ref, s):
        def body(r, carry):
            for k in range(TOP_K):
                pltpu.make_async_copy(y_hbm.at[pl.ds(p_ref[0, 0, TOP_K * r + k], 1)],
                                      ybuf.at[s, k, pl.ds(r, 1)], sems.at[s]).start()
            return carry

        lax.fori_loop(0, bt, body, 0)

    @pl.when(i == 0)
    def _():
        issue(pos_ref, 0)

    @pl.when(i + 1 < n_steps)
    def _():
        issue(posn_ref, 1 - slot)

    for k in range(TOP_K):
        pltpu.make_async_copy(y_hbm.at[pl.ds(0, bt)], ybuf.at[slot, k], sems.at[slot]).wait()
    w = w_ref[...]
    o_ref[...] = x_ref[...] + w[:, 0:1] * ybuf[slot, 0] + w[:, 1:2] * ybuf[slot, 1]


def moe_combine(x, top_w, dest, ys):
    t, d = x.shape
    bt = _pick(t, (MOE_MOVE_TOKENS, 64, 32, 16, 8))
    n_steps = t // bt
    pos = dest.reshape(n_steps, 1, TOP_K * bt)
    return pl.pallas_call(
        functools.partial(_combine_kernel, bt=bt, n_steps=n_steps),
        grid=(n_steps,),
        in_specs=[
            pl.BlockSpec((1, 1, TOP_K * bt), lambda i: (i, 0, 0), memory_space=pltpu.SMEM),
            pl.BlockSpec((1, 1, TOP_K * bt), lambda i: (jnp.minimum(i + 1, n_steps - 1), 0, 0),
                         memory_space=pltpu.SMEM),
            pl.BlockSpec((bt, d), lambda i: (i, 0)),
            pl.BlockSpec((bt, LANES), lambda i: (i, 0)),
            pl.BlockSpec(memory_space=pl.ANY),
        ],
        out_specs=pl.BlockSpec((bt, d), lambda i: (i, 0)),
        out_shape=jax.ShapeDtypeStruct((t, d), F32),
        scratch_shapes=[pltpu.VMEM((2, TOP_K, bt, d), F32), pltpu.SemaphoreType.DMA((2,))],
        compiler_params=_cparams(("arbitrary",)),
        name="moe_combine",
    )(pos, pos, x, top_w, ys)


def moe_ffn(x, gain, router, w_gu, w_down):
    t, d = x.shape
    n_e = w_down.shape[0]
    bm = min(MOE_TILE_ROWS, max(8, t // 8))
    n_tiles = (TOP_K * t + n_e * (bm - 1) + bm - 1) // bm
    top_idx, top_w = route_tokens(x, gain, router)
    dest, tile_e, n_valid, gend, cnt = _moe_plan(top_idx, n_e, bm, n_tiles)
    xs = moe_dispatch(x, dest, gend, cnt, n_valid, n_tiles, bm)
    ys = moe_expert_ffn(xs, gain, w_gu, w_down, tile_e, n_valid, bm)
    return moe_combine(x, top_w, dest, ys)


def _prep_layer(i, p):
    w_in = p["w_in"][i]
    d = w_in.shape[0]
    w_main = jnp.concatenate([w_in[:, :OFF_GATE_END], w_in[:, OFF_DECAY_END:]], axis=1).astype(BF16)
    w_decay = jnp.zeros((d, LANES), BF16).at[:, :2 * GLA_RANK].set(
        w_in[:, OFF_GATE_END:OFF_DECAY_END].astype(BF16))

    def up_mat(up, row0):
        u = up.reshape(GLA_RANK, GLA_HEADS, GLA_DK).transpose(1, 0, 2).astype(BF16)
        return jnp.zeros((GLA_HEADS, LANES, GLA_DK), BF16).at[:, row0:row0 + GLA_RANK, :].set(u)

    return dict(
        norm_mix=p["norm_mix"][i], w_main=w_main, w_decay=w_decay,
        u_f=up_mat(p["gla_up_f"][i], 0), u_b=up_mat(p["gla_up_b"][i], GLA_RANK),
        bias_f=p["gla_bias_f"][i].reshape(GLA_HEADS, 1, GLA_DK).astype(F32),
        bias_b=p["gla_bias_b"][i].reshape(GLA_HEADS, 1, GLA_DK).astype(F32),
        gla_out_norm=p["gla_out_norm"][i], swa_q_norm=p["swa_q_norm"][i], swa_k_norm=p["swa_k_norm"][i],
        swa_sink=p["swa_sink"][i], w_out=p["w_out"][i].astype(BF16),
        norm_cross=p["norm_cross"][i], norm_mem=p["norm_mem"][i],
        cx_wq=p["cx_wq"][i].astype(BF16), cx_wkv=p["cx_wkv"][i].astype(BF16),
        cx_q_norm=p["cx_q_norm"][i], cx_k_norm=p["cx_k_norm"][i], cx_wo=p["cx_wo"][i].astype(BF16),
        norm_ffn=p["norm_ffn"][i],
    )


def _trunk(x, mem, layers, bias_tab):
    b, l, d = x.shape
    t = b * l
    for i, lw in enumerate(layers):
        proj, decay_in = in_proj(x.reshape(t, d), lw["norm_mix"], lw["w_main"], lw["w_decay"])
        proj = proj.reshape(b, l, MAIN_W)
        decay_in = decay_in.reshape(b, l, LANES)
        o_f = gla_direction(proj, decay_in, lw["u_f"], lw["bias_f"], reverse=False)
        o_gla = gla_direction(proj, decay_in, lw["u_b"], lw["bias_b"], reverse=True,
                              o_fwd=o_f, out_gain=lw["gla_out_norm"])
        o_swa = swa_mixer(proj, bias_tab, lw["swa_sink"], lw["swa_q_norm"], lw["swa_k_norm"])
        x1 = out_proj(o_gla.reshape(t, GLA_V_W), o_swa.reshape(t, SWA_Q_W), lw["w_out"], x.reshape(t, d))
        kmem, vmem = cross_kv(mem, lw["norm_mem"], lw["cx_wkv"], lw["cx_k_norm"])
        x2 = cross_attention(x1.reshape(b, l, d), kmem, vmem, lw["norm_cross"], lw["cx_wq"],
                             lw["cx_q_norm"], lw["cx_wo"]).reshape(t, d)
        if i % 2 == 0:
            x3 = swiglu_ffn(x2, lw["norm_ffn"], lw["ffn_w_gu"], lw["ffn_w_down"])
        else:
            x3 = moe_ffn(x2, lw["norm_ffn"], lw["moe_router"], lw["moe_w_gu"], lw["moe_w_down"])
        x = x3.reshape(b, l, d)
    return x


def kernel(x_prompt, x_sample, mem_prompt, mem_sample, rel_bias, norm_mix, w_in, gla_up_f, gla_bias_f, gla_up_b, gla_bias_b, gla_out_norm, swa_q_norm, swa_k_norm, swa_sink, w_out, norm_cross, norm_mem, cx_wq, cx_wkv, cx_q_norm, cx_k_norm, cx_wo, norm_ffn, ffn_w_gu, ffn_w_down, moe_router, moe_w_gu, moe_w_down):
    p = dict(norm_mix=norm_mix, w_in=w_in, gla_up_f=gla_up_f, gla_bias_f=gla_bias_f, gla_up_b=gla_up_b,
             gla_bias_b=gla_bias_b, gla_out_norm=gla_out_norm, swa_q_norm=swa_q_norm, swa_k_norm=swa_k_norm,
             swa_sink=swa_sink, w_out=w_out, norm_cross=norm_cross, norm_mem=norm_mem, cx_wq=cx_wq,
             cx_wkv=cx_wkv, cx_q_norm=cx_q_norm, cx_k_norm=cx_k_norm, cx_wo=cx_wo, norm_ffn=norm_ffn)
    depth = w_in.shape[0]
    layers = []
    for i in range(depth):
        lw = _prep_layer(i, p)
        if i % 2 == 0:
            lw["ffn_w_gu"] = ffn_w_gu[i // 2][None].astype(BF16)
            lw["ffn_w_down"] = ffn_w_down[i // 2][None].astype(BF16)
        else:
            lw["moe_router"] = moe_router[i // 2]
            lw["moe_w_gu"] = moe_w_gu[i // 2].astype(BF16)
            lw["moe_w_down"] = moe_w_down[i // 2].astype(BF16)
        layers.append(lw)
    bias_tab = swa_bias_table(rel_bias)
    y_prompt = _trunk(x_prompt, mem_prompt, layers, bias_tab)
    y_sample = _trunk(x_sample, mem_sample, layers, bias_tab)
    return (y_prompt, y_sample)
```

```python
import functools

import numpy as np
import jax
import jax.numpy as jnp
from jax import lax
from jax.experimental import pallas as pl
from jax.experimental.pallas import tpu as pltpu

F32 = jnp.float32
BF16 = jnp.bfloat16

HEAD_DIM = 128
GLA_HEADS = 8
GLA_DK = 128
GLA_DV = 256
GLA_RANK = 16
GLA_TAU = 16.0
GLA_CHUNK = 64
SWA_HQ = 16
SWA_HKV = 4
SWA_GROUP = SWA_HQ // SWA_HKV
WINDOW = 128
N_BUCKETS = 32
MAX_DISTANCE = 128
CX_HEADS = 4
TOP_K = 2
EPS = 1e-6
NEG_INF = -1e30

GLA_QK_W = GLA_HEADS * GLA_DK
GLA_V_W = GLA_HEADS * GLA_DV
SWA_Q_W = SWA_HQ * HEAD_DIM
SWA_KV_W = SWA_HKV * HEAD_DIM
CX_W = CX_HEADS * HEAD_DIM
OFF_GATE_END = 2 * GLA_QK_W + 2 * GLA_V_W
OFF_DECAY_END = OFF_GATE_END + 2 * GLA_RANK
MAIN_W = OFF_GATE_END + SWA_Q_W + 2 * SWA_KV_W

LANES = 128
VMEM_LIMIT_BYTES = 56 * 1024 * 1024


def _cparams(sem):
    return pltpu.CompilerParams(dimension_semantics=sem, vmem_limit_bytes=VMEM_LIMIT_BYTES)


def _rms_rows(x, gain):
    ms = jnp.mean(x * x, axis=-1, keepdims=True)
    return x * lax.rsqrt(ms + EPS) * gain


def _norm_block_to(x_ref, g_ref, hn_ref, slab=64):
    rows = x_ref.shape[0]
    slab = min(slab, rows)

    def body(r, carry):
        r0 = pl.multiple_of(r * slab, slab)
        hn_ref[pl.ds(r0, slab), :] = _rms_rows(x_ref[pl.ds(r0, slab), :], g_ref[...]).astype(hn_ref.dtype)
        return carry

    lax.fori_loop(0, rows // slab, body, 0)


def _pick(n, cands):
    for c in cands:
        if n % c == 0:
            return c
    return n


def _in_proj_kernel(x_ref, g_ref, w_ref, wa_ref, o_ref, oa_ref, hn_ref):
    @pl.when(pl.program_id(1) == 0)
    def _():
        _norm_block_to(x_ref, g_ref, hn_ref)
        oa_ref[...] = jnp.dot(hn_ref[...], wa_ref[...], preferred_element_type=F32)

    o_ref[...] = jnp.dot(hn_ref[...], w_ref[...], preferred_element_type=F32).astype(o_ref.dtype)


def in_proj(x, gain, w_main, w_decay):
    t, d = x.shape
    n = w_main.shape[1]
    bm = _pick(t, (512, 256, 128, 64, 32, 16, 8))
    bn = _pick(n, (1024, 768, 512, 256, 128))
    return pl.pallas_call(
        _in_proj_kernel,
        grid=(t // bm, n // bn),
        in_specs=[
            pl.BlockSpec((bm, d), lambda i, j: (i, 0)),
            pl.BlockSpec((1, d), lambda i, j: (0, 0)),
            pl.BlockSpec((d, bn), lambda i, j: (0, j)),
            pl.BlockSpec((d, LANES), lambda i, j: (0, 0)),
        ],
        out_specs=[
            pl.BlockSpec((bm, bn), lambda i, j: (i, j)),
            pl.BlockSpec((bm, LANES), lambda i, j: (i, 0)),
        ],
        out_shape=[jax.ShapeDtypeStruct((t, n), BF16), jax.ShapeDtypeStruct((t, LANES), F32)],
        scratch_shapes=[pltpu.VMEM((bm, d), BF16)],
        compiler_params=_cparams(("parallel", "arbitrary")),
        name="in_proj",
    )(x, gain.reshape(1, d), w_main, w_decay)


def _log_sigmoid(z):
    return jnp.minimum(z, 0.0) - jnp.log1p(jnp.exp(-jnp.abs(z)))


def _gla_kernel(*refs, reverse, finalize, rows):
    if finalize:
        q_ref, k_ref, v_ref, a_ref, u_ref, b_ref, tri_ref, msk_ref, of_ref, g_ref, gain_ref, o_ref, st_ref = refs
    else:
        q_ref, k_ref, v_ref, a_ref, u_ref, b_ref, tri_ref, msk_ref, o_ref, st_ref = refs
    c = GLA_CHUNK
    nch = rows // c

    @pl.when(pl.program_id(2) == 0)
    def _():
        st_ref[...] = jnp.zeros_like(st_ref)

    scale = GLA_DK ** -0.5
    nt = (((1,), (1,)), ((), ()))
    tn = (((0,), (0,)), ((), ()))

    z = jnp.dot(a_ref[0].astype(BF16), u_ref[0], preferred_element_type=F32) + b_ref[0]
    la = _log_sigmoid(z) * (1.0 / GLA_TAU)

    la_hi = la.astype(BF16)
    la_lo = (la - la_hi.astype(F32)).astype(BF16)
    cum2 = jnp.dot(tri_ref[...], jnp.concatenate([la_hi, la_lo], axis=1), preferred_element_type=F32)
    cum = cum2[:, :GLA_DK] + cum2[:, GLA_DK:]
    lasts = [cum[ch * c:ch * c + 1] if reverse else cum[(ch + 1) * c - 1:(ch + 1) * c] for ch in range(nch)]
    last_rows = jnp.concatenate([jnp.broadcast_to(l, (c, GLA_DK)) for l in lasts], axis=0)

    q = q_ref[0].astype(F32)
    k = k_ref[0].astype(F32)
    v = v_ref[0]
    q_in = ((q * scale) * jnp.exp(cum)).astype(BF16)
    k_in = (k * jnp.exp(-cum)).astype(BF16)
    k_end = (k * jnp.exp(last_rows - cum)).astype(BF16)
    sc = lax.dot_general(q_in, k_in, nt, preferred_element_type=F32)
    sc = jnp.where(msk_ref[...] > 0.5, sc, 0.0).astype(BF16)
    o_intra = jnp.dot(sc, v, preferred_element_type=F32)
    kvs = [lax.dot_general(v[ch * c:(ch + 1) * c], k_end[ch * c:(ch + 1) * c], tn,
                           preferred_element_type=F32) for ch in range(nch)]

    st = st_ref[...]
    order = range(nch - 1, -1, -1) if reverse else range(nch)
    for ch in order:
        sl = slice(ch * c, (ch + 1) * c)
        o = o_intra[sl] + lax.dot_general(q_in[sl], st.astype(BF16), nt, preferred_element_type=F32)
        st = st * jnp.exp(lasts[ch]) + kvs[ch]
        if finalize:
            o = o + of_ref[0, sl, :]
            o = _rms_rows(o, gain_ref[...])
            g = g_ref[0, sl, :].astype(F32)
            o_ref[0, sl, :] = (o * (g * jax.nn.sigmoid(g))).astype(o_ref.dtype)
        else:
            o_ref[0, sl, :] = o
    st_ref[...] = st


def gla_direction(proj, decay_in, u, bias, *, reverse, o_fwd=None, out_gain=None):
    b, l, _ = proj.shape
    rows = _pick(l, (512, 256, 128, 64))
    nb = l // rows
    finalize = o_fwd is not None
    blk = (lambda s: nb - 1 - s) if reverse else (lambda s: s)
    kq = GLA_QK_W // GLA_DK
    kv = 2 * GLA_QK_W // GLA_DV
    kg = kv + GLA_HEADS
    in_specs = [
        pl.BlockSpec((1, rows, GLA_DK), lambda bi, h, s: (bi, blk(s), h)),
        pl.BlockSpec((1, rows, GLA_DK), lambda bi, h, s: (bi, blk(s), kq + h)),
        pl.BlockSpec((1, rows, GLA_DV), lambda bi, h, s: (bi, blk(s), kv + h)),
        pl.BlockSpec((1, rows, LANES), lambda bi, h, s: (bi, blk(s), 0)),
        pl.BlockSpec((1, LANES, GLA_DK), lambda bi, h, s: (h, 0, 0)),
        pl.BlockSpec((1, 1, GLA_DK), lambda bi, h, s: (h, 0, 0)),
        pl.BlockSpec((rows, rows), lambda bi, h, s: (0, 0)),
        pl.BlockSpec((rows, rows), lambda bi, h, s: (0, 0)),
    ]
    ri = np.arange(rows)[:, None]
    ci = np.arange(rows)[None, :]
    same = (ri // GLA_CHUNK) == (ci // GLA_CHUNK)
    tri = same & ((ci >= ri) if reverse else (ci <= ri))
    msk = same & ((ci > ri) if reverse else (ci <= ri))
    args = [proj, proj, proj, decay_in, u, bias, jnp.asarray(tri, BF16), jnp.asarray(msk, F32)]
    if finalize:
        in_specs += [
            pl.BlockSpec((1, rows, GLA_DV), lambda bi, h, s: (bi, blk(s), h)),
            pl.BlockSpec((1, rows, GLA_DV), lambda bi, h, s: (bi, blk(s), kg + h)),
            pl.BlockSpec((1, GLA_DV), lambda bi, h, s: (0, 0)),
        ]
        args += [o_fwd, proj, out_gain.reshape(1, GLA_DV)]
    out_dtype = BF16 if finalize else F32
    return pl.pallas_call(
        functools.partial(_gla_kernel, reverse=reverse, finalize=finalize, rows=rows),
        grid=(b, GLA_HEADS, nb),
        in_specs=in_specs,
        out_specs=pl.BlockSpec((1, rows, GLA_DV), lambda bi, h, s: (bi, blk(s), h)),
        out_shape=jax.ShapeDtypeStruct((b, l, GLA_V_W), out_dtype),
        scratch_shapes=[pltpu.VMEM((GLA_DV, GLA_DK), F32)],
        compiler_params=_cparams(("parallel", "parallel", "arbitrary")),
        name="gla_bwd" if reverse else "gla_fwd",
    )(*args)


def _t5_buckets(rel):
    half = N_BUCKETS // 2
    ret = (rel > 0).astype(np.int32) * half
    n = np.abs(rel)
    max_exact = half // 2
    large = max_exact + (np.log(np.maximum(n, 1) / max_exact) / np.log(MAX_DISTANCE / max_exact)
                         * (half - max_exact)).astype(np.int32)
    large = np.minimum(large, half - 1)
    return ret + np.where(n < max_exact, n, large)


def _bias_table_kernel(rb_ref, bkt_ref, o_ref):
    h = pl.program_id(0)
    bkt = bkt_ref[...]
    for g in range(SWA_GROUP):
        acc = jnp.zeros(bkt.shape, F32)
        for b in range(N_BUCKETS):
            acc = jnp.where(bkt == b, rb_ref[b, h * SWA_GROUP + g], acc)
        o_ref[0, g * WINDOW:(g + 1) * WINDOW, :] = acc


def swa_bias_table(rel_bias):
    rel = np.arange(3 * WINDOW)[None, :] - WINDOW - np.arange(WINDOW)[:, None]
    bkt = jnp.asarray(_t5_buckets(rel), jnp.int32)
    return pl.pallas_call(
        _bias_table_kernel,
        grid=(SWA_HKV,),
        in_specs=[
            pl.BlockSpec(memory_space=pltpu.SMEM),
            pl.BlockSpec((WINDOW, 3 * WINDOW), lambda h: (0, 0)),
        ],
        out_specs=pl.BlockSpec((1, SWA_GROUP * WINDOW, 3 * WINDOW), lambda h: (h, 0, 0)),
        out_shape=jax.ShapeDtypeStruct((SWA_HKV, SWA_GROUP * WINDOW, 3 * WINDOW), F32),
        compiler_params=_cparams(("parallel",)),
        name="swa_bias_table",
    )(rel_bias.astype(F32), bkt)


def _swa_kernel(sink_ref, q_ref, kp_ref, kc_ref, kn_ref, vp_ref, vc_ref, vn_ref, bias_ref,
                qg_ref, kg_ref, o_ref, *, nb):
    i = pl.program_id(1)
    w = WINDOW
    gw = SWA_GROUP * HEAD_DIM
    row = lax.broadcasted_iota(jnp.int32, (w, 3 * w), 0)
    col = lax.broadcasted_iota(jnp.int32, (w, 3 * w), 1)
    rel = col - w - row
    valid = (jnp.abs(rel) <= w) & ((col >= w) | (i > 0)) & ((col < 2 * w) | (i < nb - 1))
    valid = jnp.concatenate([valid] * SWA_GROUP, axis=0)
    for h in range(SWA_HKV):
        hs = slice(h * HEAD_DIM, (h + 1) * HEAD_DIM)
        q = q_ref[0, :, h * gw:(h + 1) * gw]
        qs = jnp.concatenate([q[:, g * HEAD_DIM:(g + 1) * HEAD_DIM] for g in range(SWA_GROUP)], axis=0)
        qn = _rms_rows(qs.astype(F32), qg_ref[...]).astype(BF16)
        k = jnp.concatenate([kp_ref[0, :, hs], kc_ref[0, :, hs], kn_ref[0, :, hs]], axis=0).astype(F32)
        kn = _rms_rows(k, kg_ref[...]).astype(BF16)
        v = jnp.concatenate([vp_ref[0, :, hs], vc_ref[0, :, hs], vn_ref[0, :, hs]], axis=0)
        s = lax.dot_general(qn, kn, (((1,), (1,)), ((), ())), preferred_element_type=F32)
        logits = s * (HEAD_DIM ** -0.5) + bias_ref[h]
        logits = jnp.where(valid, logits, NEG_INF)
        sink = jnp.concatenate(
            [jnp.full((w, 1), sink_ref[h * SWA_GROUP + g], F32) for g in range(SWA_GROUP)], axis=0)
        m = jnp.maximum(jnp.max(logits, axis=-1, keepdims=True), sink)
        p = jnp.exp(logits - m)
        denom = jnp.sum(p, axis=-1, keepdims=True) + jnp.exp(sink - m)
        o = jnp.dot(p.astype(BF16), v, preferred_element_type=F32) / denom
        o_ref[0, :, h * gw:(h + 1) * gw] = jnp.concatenate(
            [o[g * w:(g + 1) * w] for g in range(SWA_GROUP)], axis=1).astype(o_ref.dtype)


def swa_mixer(proj, bias_tab, sink, q_gain, k_gain):
    b, l, _ = proj.shape
    nb = l // WINDOW
    q0 = OFF_GATE_END // SWA_Q_W
    k0 = (OFF_GATE_END + SWA_Q_W) // SWA_KV_W
    v0 = k0 + 1
    assert OFF_GATE_END % SWA_Q_W == 0 and (OFF_GATE_END + SWA_Q_W) % SWA_KV_W == 0

    def kv_spec(base, shift):
        return pl.BlockSpec(
            (1, WINDOW, SWA_KV_W),
            lambda bi, i: (bi, jnp.clip(i + shift, 0, nb - 1), base))

    return pl.pallas_call(
        functools.partial(_swa_kernel, nb=nb),
        grid=(b, nb),
        in_specs=[
            pl.BlockSpec(memory_space=pltpu.SMEM),
            pl.BlockSpec((1, WINDOW, SWA_Q_W), lambda bi, i: (bi, i, q0)),
            kv_spec(k0, -1), kv_spec(k0, 0), kv_spec(k0, 1),
            kv_spec(v0, -1), kv_spec(v0, 0), kv_spec(v0, 1),
            pl.BlockSpec((SWA_HKV, SWA_GROUP * WINDOW, 3 * WINDOW), lambda bi, i: (0, 0, 0)),
            pl.BlockSpec((1, HEAD_DIM), lambda bi, i: (0, 0)),
            pl.BlockSpec((1, HEAD_DIM), lambda bi, i: (0, 0)),
        ],
        out_specs=pl.BlockSpec((1, WINDOW, SWA_Q_W), lambda bi, i: (bi, i, 0)),
        out_shape=jax.ShapeDtypeStruct((b, l, SWA_Q_W), BF16),
        compiler_params=_cparams(("parallel", "arbitrary")),
        name="swa",
    )(sink.astype(F32), proj, proj, proj, proj, proj, proj, proj, bias_tab,
      q_gain.reshape(1, HEAD_DIM), k_gain.reshape(1, HEAD_DIM))


def _out_proj_kernel(a_ref, b_ref, wa_ref, wb_ref, x_ref, o_ref):
    o_ref[...] = (x_ref[...]
                  + jnp.dot(a_ref[...], wa_ref[...], preferred_element_type=F32)
                  + jnp.dot(b_ref[...], wb_ref[...], preferred_element_type=F32))


def out_proj(o_gla, o_swa, w_out, x):
    t, d = x.shape
    ka, kb = o_gla.shape[1], o_swa.shape[1]
    bm = _pick(t, (1024, 512, 256, 128, 64, 32, 16, 8))
    bn = _pick(d, (1024, 512, 256, 128))
    assert ka == kb and w_out.shape[0] == ka + kb
    return pl.pallas_call(
        _out_proj_kernel,
        grid=(t // bm, d // bn),
        in_specs=[
            pl.BlockSpec((bm, ka), lambda i, j: (i, 0)),
            pl.BlockSpec((bm, kb), lambda i, j: (i, 0)),
            pl.BlockSpec((ka, bn), lambda i, j: (0, j)),
            pl.BlockSpec((kb, bn), lambda i, j: (1, j)),
            pl.BlockSpec((bm, bn), lambda i, j: (i, j)),
        ],
        out_specs=pl.BlockSpec((bm, bn), lambda i, j: (i, j)),
        out_shape=jax.ShapeDtypeStruct((t, d), F32),
        compiler_params=_cparams(("parallel", "arbitrary")),
        name="out_proj",
    )(o_gla, o_swa, w_out, w_out, x)


def _cross_kv_kernel(m_ref, g_ref, w_ref, kg_ref, k_ref, v_ref, hn_ref):
    _norm_block_to(m_ref.at[0], g_ref, hn_ref)
    kv = jnp.dot(hn_ref[...], w_ref[...], preferred_element_type=F32)
    for h in range(CX_HEADS):
        kh = kv[:, h * HEAD_DIM:(h + 1) * HEAD_DIM]
        k_ref[0, :, h * HEAD_DIM:(h + 1) * HEAD_DIM] = _rms_rows(kh, kg_ref[...]).astype(k_ref.dtype)
    v_ref[0] = kv[:, CX_W:].astype(v_ref.dtype)


def cross_kv(mem, g_m, wkv, k_gain):
    b, n_mem, d = mem.shape
    return pl.pallas_call(
        _cross_kv_kernel,
        grid=(b,),
        in_specs=[
            pl.BlockSpec((1, n_mem, d), lambda i: (i, 0, 0)),
            pl.BlockSpec((1, d), lambda i: (0, 0)),
            pl.BlockSpec((d, 2 * CX_W), lambda i: (0, 0)),
            pl.BlockSpec((1, HEAD_DIM), lambda i: (0, 0)),
        ],
        out_specs=[
            pl.BlockSpec((1, n_mem, CX_W), lambda i: (i, 0, 0)),
            pl.BlockSpec((1, n_mem, CX_W), lambda i: (i, 0, 0)),
        ],
        out_shape=[jax.ShapeDtypeStruct((b, n_mem, CX_W), BF16)] * 2,
        scratch_shapes=[pltpu.VMEM((n_mem, d), BF16)],
        compiler_params=_cparams(("parallel",)),
        name="cross_kv",
    )(mem, g_m.reshape(1, d), wkv, k_gain.reshape(1, HEAD_DIM))


def _cross_kernel(x_ref, g_ref, wq_ref, qg_ref, k_ref, v_ref, wo_ref, o_ref, hn_ref):
    _norm_block_to(x_ref.at[0], g_ref, hn_ref)
    q = jnp.dot(hn_ref[...], wq_ref[...], preferred_element_type=F32)
    outs = []
    for h in range(CX_HEADS):
        hs = slice(h * HEAD_DIM, (h + 1) * HEAD_DIM)
        qh = _rms_rows(q[:, hs], qg_ref[...]).astype(BF16)
        s = lax.dot_general(qh, k_ref[0, :, hs], (((1,), (1,)), ((), ())),
                            preferred_element_type=F32) * (HEAD_DIM ** -0.5)
        m = jnp.max(s, axis=-1, keepdims=True)
        p = jnp.exp(s - m)
        denom = jnp.sum(p, axis=-1, keepdims=True)
        outs.append(jnp.dot(p.astype(BF16), v_ref[0, :, hs], preferred_element_type=F32) / denom)
    o = jnp.concatenate(outs, axis=1).astype(BF16)
    o_ref[0] = x_ref[0] + jnp.dot(o, wo_ref[...], preferred_element_type=F32)


def cross_attention(x, kmem, vmem, g_x, wq, q_gain, wo):
    b, l, d = x.shape
    n_mem = kmem.shape[1]
    bm = _pick(l, (512, 256, 128, 64, 32, 16, 8))
    return pl.pallas_call(
        _cross_kernel,
        grid=(b, l // bm),
        in_specs=[
            pl.BlockSpec((1, bm, d), lambda bi, i: (bi, i, 0)),
            pl.BlockSpec((1, d), lambda bi, i: (0, 0)),
            pl.BlockSpec((d, CX_W), lambda bi, i: (0, 0)),
            pl.BlockSpec((1, HEAD_DIM), lambda bi, i: (0, 0)),
            pl.BlockSpec((1, n_mem, CX_W), lambda bi, i: (bi, 0, 0)),
            pl.BlockSpec((1, n_mem, CX_W), lambda bi, i: (bi, 0, 0)),
            pl.BlockSpec((CX_W, d), lambda bi, i: (0, 0)),
        ],
        out_specs=pl.BlockSpec((1, bm, d), lambda bi, i: (bi, i, 0)),
        out_shape=jax.ShapeDtypeStruct((b, l, d), F32),
        scratch_shapes=[pltpu.VMEM((bm, d), BF16)],
        compiler_params=_cparams(("parallel", "arbitrary")),
        name="cross_attention",
    )(x, g_x.reshape(1, d), wq, q_gain.reshape(1, HEAD_DIM), kmem, vmem, wo)


def _router_kernel(x_ref, g_ref, rhi_ref, rlo_ref, idx_ref, w_ref, *, n_experts):
    hn = _rms_rows(x_ref[...], g_ref[...])
    hi = hn.astype(BF16)
    lo = (hn - hi.astype(F32)).astype(BF16)
    logits = (jnp.dot(hi, rhi_ref[...], preferred_element_type=F32)
              + jnp.dot(hi, rlo_ref[...], preferred_element_type=F32)
              + jnp.dot(lo, rhi_ref[...], preferred_element_type=F32))
    lane = lax.broadcasted_iota(jnp.int32, logits.shape, 1).astype(F32)
    low = jnp.float32(-3.0e38)
    l1 = jnp.where(lane < n_experts, logits, low)
    m1 = jnp.max(l1, axis=-1, keepdims=True)
    i1 = jnp.min(jnp.where(l1 == m1, lane, float(LANES)), axis=-1, keepdims=True)
    l2 = jnp.where(lane == i1, low, l1)
    m2 = jnp.max(l2, axis=-1, keepdims=True)
    i2 = jnp.min(jnp.where(l2 == m2, lane, float(LANES)), axis=-1, keepdims=True)
    e2 = jnp.exp(m2 - m1)
    w1 = 1.0 / (1.0 + e2)
    w2 = e2 / (1.0 + e2)
    idx_ref[...] = jnp.where(lane == 0.0, i1, jnp.where(lane == 1.0, i2, 0.0)).astype(jnp.int32)
    w_ref[...] = jnp.where(lane == 0.0, w1, jnp.where(lane == 1.0, w2, 0.0))


def route_tokens(x, gain, router):
    t, d = x.shape
    n_experts = router.shape[1]
    r = jnp.zeros((d, LANES), F32).at[:, :n_experts].set(router.astype(F32))
    r_hi = r.astype(BF16)
    r_lo = (r - r_hi.astype(F32)).astype(BF16)
    bm = _pick(t, (256, 128, 64, 32, 16, 8))
    return pl.pallas_call(
        functools.partial(_router_kernel, n_experts=n_experts),
        grid=(t // bm,),
        in_specs=[
            pl.BlockSpec((bm, d), lambda i: (i, 0)),
            pl.BlockSpec((1, d), lambda i: (0, 0)),
            pl.BlockSpec((d, LANES), lambda i: (0, 0)),
            pl.BlockSpec((d, LANES), lambda i: (0, 0)),
        ],
        out_specs=[pl.BlockSpec((bm, LANES), lambda i: (i, 0)), pl.BlockSpec((bm, LANES), lambda i: (i, 0))],
        out_shape=[jax.ShapeDtypeStruct((t, LANES), jnp.int32), jax.ShapeDtypeStruct((t, LANES), F32)],
        compiler_params=_cparams(("parallel",)),
        name="moe_router",
    )(x, gain.reshape(1, d), r_hi, r_lo)


def _swiglu_step(hn, wg_ref, wu_ref, wd_ref):
    a = jnp.dot(hn, wg_ref[0], preferred_element_type=F32)
    b = jnp.dot(hn, wu_ref[0], preferred_element_type=F32)
    hid = (a * jax.nn.sigmoid(a)) * b
    return jnp.dot(hid.astype(BF16), wd_ref[0], preferred_element_type=F32)


def _ffn_kernel(x_ref, g_ref, wg_ref, wu_ref, wd_ref, o_ref, hn_ref):
    @pl.when(pl.program_id(1) == 0)
    def _():
        _norm_block_to(x_ref, g_ref, hn_ref)
        o_ref[...] = x_ref[...]

    o_ref[...] += _swiglu_step(hn_ref[...], wg_ref, wu_ref, wd_ref)


def swiglu_ffn(x, gain, w_gu, w_down):
    t, d = x.shape
    ffn = w_down.shape[1]
    bm = _pick(t, (512, 256, 128, 64, 32, 16, 8))
    tf = _pick(ffn, (256, 128))
    nf = ffn // tf
    return pl.pallas_call(
        _ffn_kernel,
        grid=(t // bm, nf),
        in_specs=[
            pl.BlockSpec((bm, d), lambda i, f: (i, 0)),
            pl.BlockSpec((1, d), lambda i, f: (0, 0)),
            pl.BlockSpec((1, d, tf), lambda i, f: (0, 0, f)),
            pl.BlockSpec((1, d, tf), lambda i, f: (0, 0, nf + f)),
            pl.BlockSpec((1, tf, d), lambda i, f: (0, f, 0)),
        ],
        out_specs=pl.BlockSpec((bm, d), lambda i, f: (i, 0)),
        out_shape=jax.ShapeDtypeStruct((t, d), F32),
        scratch_shapes=[pltpu.VMEM((bm, d), BF16)],
        compiler_params=_cparams(("parallel", "arbitrary")),
        name="dense_ffn",
    )(x, gain.reshape(1, d), w_gu, w_gu, w_down)


MOE_TILE_ROWS = 512
MOE_MOVE_TOKENS = 128
MOE_ZERO_ROWS = 128


def _moe_plan(top_idx, n_e, bm, n_tiles):
    e_flat = top_idx[:, :TOP_K].reshape(-1)
    onehot = (e_flat[:, None] == jnp.arange(n_e, dtype=jnp.int32)[None, :]).astype(jnp.int32)
    csum = jnp.cumsum(onehot, axis=0)
    cnt = csum[-1]
    rank = jnp.sum(onehot * (csum - 1), axis=1)
    padded = ((cnt + bm - 1) // bm) * bm
    gend = jnp.cumsum(padded)
    gstart = gend - padded
    dest = (jnp.sum(onehot * gstart[None, :], axis=1) + rank).astype(jnp.int32)
    n_valid = (gend[-1] // bm).astype(jnp.int32)
    tile_start = jnp.arange(n_tiles, dtype=jnp.int32) * bm
    tile_e = jnp.sum((tile_start[:, None] >= gend[None, :]).astype(jnp.int32), axis=1)
    tile_e = jnp.minimum(tile_e, n_e - 1)
    last_e = jnp.sum(jnp.where(jnp.arange(n_tiles) == n_valid - 1, tile_e, 0))
    tile_e = jnp.where(jnp.arange(n_tiles) < n_valid, tile_e, last_e).astype(jnp.int32)
    return dest, tile_e, n_valid.reshape(1), gend.astype(jnp.int32), cnt.astype(jnp.int32)


def _dispatch_kernel(gend_ref, cnt_ref, nv_ref, dest_ref, x_ref, xs_hbm, zbuf, sem, zsem, *,
                     bt, bm, n_e, n_tiles):
    i = pl.program_id(0)

    @pl.when(i == 0)
    def _():
        zbuf[...] = jnp.zeros_like(zbuf)
        zr = zbuf.shape[0]

        def clear_tile(base):
            copies = [pltpu.make_async_copy(
                zbuf, xs_hbm.at[pl.ds(pl.multiple_of(base + z * zr, zr), zr)], zsem)
                for z in range(bm // zr)]
            for cp in copies:
                cp.start()
            for cp in copies:
                cp.wait()

        for e in range(n_e):
            @pl.when(cnt_ref[e] > 0)
            def _():
                clear_tile(gend_ref[e] - bm)

        for back in range(1, min(n_e, n_tiles) + 1):
            @pl.when(n_tiles - back >= nv_ref[0])
            def _():
                clear_tile((n_tiles - back) * bm)

    def issue(r, carry):
        for k in range(TOP_K):
            pltpu.make_async_copy(x_ref.at[pl.ds(r, 1)],
                                  xs_hbm.at[pl.ds(dest_ref[0, 0, TOP_K * r + k], 1)], sem).start()
        return carry

    lax.fori_loop(0, bt, issue, 0)
    for k in range(TOP_K):
        pltpu.make_async_copy(x_ref, xs_hbm.at[pl.ds(0, bt)], sem).wait()


def moe_dispatch(x, dest, gend, cnt, n_valid, n_tiles, bm):
    t, d = x.shape
    n_rows = n_tiles * bm
    bt = _pick(t, (MOE_MOVE_TOKENS, 64, 32, 16, 8))
    n_steps = t // bt
    zr = min(MOE_ZERO_ROWS, bm)
    return pl.pallas_call(
        functools.partial(_dispatch_kernel, bt=bt, bm=bm, n_e=cnt.shape[0], n_tiles=n_tiles),
        grid_spec=pltpu.PrefetchScalarGridSpec(
            num_scalar_prefetch=3,
            grid=(n_steps,),
            in_specs=[
                pl.BlockSpec((1, 1, TOP_K * bt), lambda i, ge, cn, nv: (i, 0, 0), memory_space=pltpu.SMEM),
                pl.BlockSpec((bt, d), lambda i, ge, cn, nv: (i, 0)),
            ],
            out_specs=pl.BlockSpec(memory_space=pl.ANY),
            scratch_shapes=[pltpu.VMEM((zr, d), F32), pltpu.SemaphoreType.DMA(()),
                            pltpu.SemaphoreType.DMA(())],
        ),
        out_shape=jax.ShapeDtypeStruct((n_rows, d), F32),
        compiler_params=_cparams(("arbitrary",)),
        name="moe_dispatch",
    )(gend, cnt, n_valid, dest.reshape(n_steps, 1, TOP_K * bt), x)


def _moe_ffn_kernel(te_ref, nv_ref, x_ref, g_ref, wg_ref, wu_ref, wd_ref, o_ref, hn_ref, *, nf):
    j = pl.program_id(0)
    f = pl.program_id(1)
    valid = j < nv_ref[0]

    @pl.when(valid & (f == 0))
    def _():
        _norm_block_to(x_ref, g_ref, hn_ref)
        o_ref[...] = _swiglu_step(hn_ref[...], wg_ref, wu_ref, wd_ref)

    @pl.when(valid & (f > 0))
    def _():
        o_ref[...] += _swiglu_step(hn_ref[...], wg_ref, wu_ref, wd_ref)

    @pl.when(jnp.logical_not(valid) & (f == nf - 1))
    def _():
        o_ref[...] = jnp.zeros_like(o_ref)


def moe_expert_ffn(xs, gain, w_gu, w_down, tile_e, n_valid, bm):
    n_rows, d = xs.shape
    n_tiles = n_rows // bm
    ffn = w_down.shape[1]
    tf = _pick(ffn, (256, 128))
    nf = ffn // tf

    def fidx(j, f, nv):
        return jnp.where(j < nv[0], f, nf - 1)

    return pl.pallas_call(
        functools.partial(_moe_ffn_kernel, nf=nf),
        grid_spec=pltpu.PrefetchScalarGridSpec(
            num_scalar_prefetch=2,
            grid=(n_tiles, nf),
            in_specs=[
                pl.BlockSpec((bm, d), lambda j, f, te, nv: (jnp.minimum(j, nv[0] - 1), 0)),
                pl.BlockSpec((1, d), lambda j, f, te, nv: (0, 0)),
                pl.BlockSpec((1, d, tf), lambda j, f, te, nv: (te[j], 0, fidx(j, f, nv))),
                pl.BlockSpec((1, d, tf), lambda j, f, te, nv: (te[j], 0, nf + fidx(j, f, nv))),
                pl.BlockSpec((1, tf, d), lambda j, f, te, nv: (te[j], fidx(j, f, nv), 0)),
            ],
            out_specs=pl.BlockSpec((bm, d), lambda j, f, te, nv: (j, 0)),
            scratch_shapes=[pltpu.VMEM((bm, d), BF16)],
        ),
        out_shape=jax.ShapeDtypeStruct((n_rows, d), F32),
        compiler_params=_cparams(("arbitrary", "arbitrary")),
        name="moe_ffn",
    )(tile_e, n_valid, xs, gain.reshape(1, d), w_gu, w_gu, w_down)


def _combine_kernel(pos_ref, posn_ref, x_ref, w_ref, y_hbm, o_ref, ybuf, sems, *, bt, n_steps):
    i = pl.program_id(0)
    slot = i % 2

    def issue(p_ref, s):
        def body(r, carry):
            for k in range(TOP_K):
                pltpu.make_async_copy(y_hbm.at[pl.ds(p_ref[0, 0, TOP_K * r + k], 1)],
                                      ybuf.at[s, k, pl.ds(r, 1)], sems.at[s]).start()
            return carry

        lax.fori_loop(0, bt, body, 0)

    @pl.when(i == 0)
    def _():
        issue(pos_ref, 0)

    @pl.when(i + 1 < n_steps)
    def _():
        issue(posn_ref, 1 - slot)

    for k in range(TOP_K):
        pltpu.make_async_copy(y_hbm.at[pl.ds(0, bt)], ybuf.at[slot, k], sems.at[slot]).wait()
    w = w_ref[...]
    o_ref[...] = x_ref[...] + w[:, 0:1] * ybuf[slot, 0] + w[:, 1:2] * ybuf[slot, 1]


def moe_combine(x, top_w, dest, ys):
    t, d = x.shape
    bt = _pick(t, (MOE_MOVE_TOKENS, 64, 32, 16, 8))
    n_steps = t // bt
    pos = dest.reshape(n_steps, 1, TOP_K * bt)
    return pl.pallas_call(
        functools.partial(_combine_kernel, bt=bt, n_steps=n_steps),
        grid=(n_steps,),
        in_specs=[
            pl.BlockSpec((1, 1, TOP_K * bt), lambda i: (i, 0, 0), memory_space=pltpu.SMEM),
            pl.BlockSpec((1, 1, TOP_K * bt), lambda i: (jnp.minimum(i + 1, n_steps - 1), 0, 0),
                         memory_space=pltpu.SMEM),
            pl.BlockSpec((bt, d), lambda i: (i, 0)),
            pl.BlockSpec((bt, LANES), lambda i: (i, 0)),
            pl.BlockSpec(memory_space=pl.ANY),
        ],
        out_specs=pl.BlockSpec((bt, d), lambda i: (i, 0)),
        out_shape=jax.ShapeDtypeStruct((t, d), F32),
        scratch_shapes=[pltpu.VMEM((2, TOP_K, bt, d), F32), pltpu.SemaphoreType.DMA((2,))],
        compiler_params=_cparams(("arbitrary",)),
        name="moe_combine",
    )(pos, pos, x, top_w, ys)


def moe_ffn(x, gain, router, w_gu, w_down):
    t, d = x.shape
    n_e = w_down.shape[0]
    bm = min(MOE_TILE_ROWS, max(8, t // 8))
    n_tiles = (TOP_K * t + n_e * (bm - 1) + bm - 1) // bm
    top_idx, top_w = route_tokens(x, gain, router)
    dest, tile_e, n_valid, gend, cnt = _moe_plan(top_idx, n_e, bm, n_tiles)
    xs = moe_dispatch(x, dest, gend, cnt, n_valid, n_tiles, bm)
    ys = moe_expert_ffn(xs, gain, w_gu, w_down, tile_e, n_valid, bm)
    return moe_combine(x, top_w, dest, ys)


def _prep_layer(i, p):
    w_in = p["w_in"][i]
    d = w_in.shape[0]
    w_main = jnp.concatenate([w_in[:, :OFF_GATE_END], w_in[:, OFF_DECAY_END:]], axis=1).astype(BF16)
    w_decay = jnp.zeros((d, LANES), BF16).at[:, :2 * GLA_RANK].set(
        w_in[:, OFF_GATE_END:OFF_DECAY_END].astype(BF16))

    def up_mat(up, row0):
        u = up.reshape(GLA_RANK, GLA_HEADS, GLA_DK).transpose(1, 0, 2).astype(BF16)
        return jnp.zeros((GLA_HEADS, LANES, GLA_DK), BF16).at[:, row0:row0 + GLA_RANK, :].set(u)

    return dict(
        norm_mix=p["norm_mix"][i], w_main=w_main, w_decay=w_decay,
        u_f=up_mat(p["gla_up_f"][i], 0), u_b=up_mat(p["gla_up_b"][i], GLA_RANK),
        bias_f=p["gla_bias_f"][i].reshape(GLA_HEADS, 1, GLA_DK).astype(F32),
        bias_b=p["gla_bias_b"][i].reshape(GLA_HEADS, 1, GLA_DK).astype(F32),
        gla_out_norm=p["gla_out_norm"][i], swa_q_norm=p["swa_q_norm"][i], swa_k_norm=p["swa_k_norm"][i],
        swa_sink=p["swa_sink"][i], w_out=p["w_out"][i].astype(BF16),
        norm_cross=p["norm_cross"][i], norm_mem=p["norm_mem"][i],
        cx_wq=p["cx_wq"][i].astype(BF16), cx_wkv=p["cx_wkv"][i].astype(BF16),
        cx_q_norm=p["cx_q_norm"][i], cx_k_norm=p["cx_k_norm"][i], cx_wo=p["cx_wo"][i].astype(BF16),
        norm_ffn=p["norm_ffn"][i],
    )


def _trunk(x, mem, layers, bias_tab):
    b, l, d = x.shape
    t = b * l
    for i, lw in enumerate(layers):
        proj, decay_in = in_proj(x.reshape(t, d), lw["norm_mix"], lw["w_main"], lw["w_decay"])
        proj = proj.reshape(b, l, MAIN_W)
        decay_in = decay_in.reshape(b, l, LANES)
        o_f = gla_direction(proj, decay_in, lw["u_f"], lw["bias_f"], reverse=False)
        o_gla = gla_direction(proj, decay_in, lw["u_b"], lw["bias_b"], reverse=True,
                              o_fwd=o_f, out_gain=lw["gla_out_norm"])
        o_swa = swa_mixer(proj, bias_tab, lw["swa_sink"], lw["swa_q_norm"], lw["swa_k_norm"])
        x1 = out_proj(o_gla.reshape(t, GLA_V_W), o_swa.reshape(t, SWA_Q_W), lw["w_out"], x.reshape(t, d))
        kmem, vmem = cross_kv(mem, lw["norm_mem"], lw["cx_wkv"], lw["cx_k_norm"])
        x2 = cross_attention(x1.reshape(b, l, d), kmem, vmem, lw["norm_cross"], lw["cx_wq"],
                             lw["cx_q_norm"], lw["cx_wo"]).reshape(t, d)
        if i % 2 == 0:
            x3 = swiglu_ffn(x2, lw["norm_ffn"], lw["ffn_w_gu"], lw["ffn_w_down"])
        else:
            x3 = moe_ffn(x2, lw["norm_ffn"], lw["moe_router"], lw["moe_w_gu"], lw["moe_w_down"])
        x = x3.reshape(b, l, d)
    return x


def kernel(x_prompt, x_sample, mem_prompt, mem_sample, rel_bias, norm_mix, w_in, gla_up_f, gla_bias_f, gla_up_b, gla_bias_b, gla_out_norm, swa_q_norm, swa_k_norm, swa_sink, w_out, norm_cross, norm_mem, cx_wq, cx_wkv, cx_q_norm, cx_k_norm, cx_wo, norm_ffn, ffn_w_gu, ffn_w_down, moe_router, moe_w_gu, moe_w_down):
    p = dict(norm_mix=norm_mix, w_in=w_in, gla_up_f=gla_up_f, gla_bias_f=gla_bias_f, gla_up_b=gla_up_b,
             gla_bias_b=gla_bias_b, gla_out_norm=gla_out_norm, swa_q_norm=swa_q_norm, swa_k_norm=swa_k_norm,
             swa_sink=swa_sink, w_out=w_out, norm_cross=norm_cross, norm_mem=norm_mem, cx_wq=cx_wq,
             cx_wkv=cx_wkv, cx_q_norm=cx_q_norm, cx_k_norm=cx_k_norm, cx_wo=cx_wo, norm_ffn=norm_ffn)
    depth = w_in.shape[0]
    layers = []
    for i in range(depth):
        lw = _prep_layer(i, p)
        if i % 2 == 0:
            lw["ffn_w_gu"] = ffn_w_gu[i // 2][None].astype(BF16)
            lw["ffn_w_down"] = ffn_w_down[i // 2][None].astype(BF16)
        else:
            lw["moe_router"] = moe_router[i // 2]
            lw["moe_w_gu"] = moe_w_gu[i // 2].astype(BF16)
            lw["moe_w_down"] = moe_w_down[i // 2].astype(BF16)
        layers.append(lw)
    bias_tab = swa_bias_table(rel_bias)
    y_prompt = _trunk(x_prompt, mem_prompt, layers, bias_tab)
    y_sample = _trunk(x_sample, mem_sample, layers, bias_tab)
    return (y_prompt, y_sample)
```

```python
import functools

import numpy as np
import jax
import jax.numpy as jnp
from jax import lax
from jax.experimental import pallas as pl
from jax.experimental.pallas import tpu as pltpu

F32 = jnp.float32
BF16 = jnp.bfloat16

HEAD_DIM = 128
GLA_HEADS = 8
GLA_DK = 128
GLA_DV = 256
GLA_RANK = 16
GLA_TAU = 16.0
GLA_CHUNK = 64
SWA_HQ = 16
SWA_HKV = 4
SWA_GROUP = SWA_HQ // SWA_HKV
WINDOW = 128
N_BUCKETS = 32
MAX_DISTANCE = 128
CX_HEADS = 4
TOP_K = 2
EPS = 1e-6
NEG_INF = -1e30

GLA_QK_W = GLA_HEADS * GLA_DK
GLA_V_W = GLA_HEADS * GLA_DV
SWA_Q_W = SWA_HQ * HEAD_DIM
SWA_KV_W = SWA_HKV * HEAD_DIM
CX_W = CX_HEADS * HEAD_DIM
OFF_GATE_END = 2 * GLA_QK_W + 2 * GLA_V_W
OFF_DECAY_END = OFF_GATE_END + 2 * GLA_RANK
MAIN_W = OFF_GATE_END + SWA_Q_W + 2 * SWA_KV_W

LANES = 128
VMEM_LIMIT_BYTES = 56 * 1024 * 1024


def _cparams(sem):
    return pltpu.CompilerParams(dimension_semantics=sem, vmem_limit_bytes=VMEM_LIMIT_BYTES)


def _rms_rows(x, gain):
    ms = jnp.mean(x * x, axis=-1, keepdims=True)
    return x * lax.rsqrt(ms + EPS) * gain


def _norm_block_to(x_ref, g_ref, hn_ref, slab=64):
    rows = x_ref.shape[0]
    slab = min(slab, rows)

    def body(r, carry):
        r0 = pl.multiple_of(r * slab, slab)
        hn_ref[pl.ds(r0, slab), :] = _rms_rows(x_ref[pl.ds(r0, slab), :], g_ref[...]).astype(hn_ref.dtype)
        return carry

    lax.fori_loop(0, rows // slab, body, 0)


def _pick(n, cands):
    for c in cands:
        if n % c == 0:
            return c
    return n


def _in_proj_kernel(x_ref, g_ref, w_ref, wa_ref, o_ref, oa_ref, hn_ref):
    @pl.when(pl.program_id(1) == 0)
    def _():
        _norm_block_to(x_ref, g_ref, hn_ref)
        oa_ref[...] = jnp.dot(hn_ref[...], wa_ref[...], preferred_element_type=F32)

    o_ref[...] = jnp.dot(hn_ref[...], w_ref[...], preferred_element_type=F32).astype(o_ref.dtype)


def in_proj(x, gain, w_main, w_decay):
    t, d = x.shape
    n = w_main.shape[1]
    bm = _pick(t, (512, 256, 128, 64, 32, 16, 8))
    bn = _pick(n, (1024, 768, 512, 256, 128))
    return pl.pallas_call(
        _in_proj_kernel,
        grid=(t // bm, n // bn),
        in_specs=[
            pl.BlockSpec((bm, d), lambda i, j: (i, 0)),
            pl.BlockSpec((1, d), lambda i, j: (0, 0)),
            pl.BlockSpec((d, bn), lambda i, j: (0, j)),
            pl.BlockSpec((d, LANES), lambda i, j: (0, 0)),
        ],
        out_specs=[
            pl.BlockSpec((bm, bn), lambda i, j: (i, j)),
            pl.BlockSpec((bm, LANES), lambda i, j: (i, 0)),
        ],
        out_shape=[jax.ShapeDtypeStruct((t, n), BF16), jax.ShapeDtypeStruct((t, LANES), F32)],
        scratch_shapes=[pltpu.VMEM((bm, d), BF16)],
        compiler_params=_cparams(("parallel", "arbitrary")),
        name="in_proj",
    )(x, gain.reshape(1, d), w_main, w_decay)


def _log_sigmoid(z):
    return jnp.minimum(z, 0.0) - jnp.log1p(jnp.exp(-jnp.abs(z)))


def _gla_kernel(*refs, reverse, finalize, rows):
    if finalize:
        q_ref, k_ref, v_ref, a_ref, u_ref, b_ref, tri_ref, msk_ref, of_ref, g_ref, gain_ref, o_ref, st_ref = refs
    else:
        q_ref, k_ref, v_ref, a_ref, u_ref, b_ref, tri_ref, msk_ref, o_ref, st_ref = refs
    c = GLA_CHUNK
    nch = rows // c

    @pl.when(pl.program_id(2) == 0)
    def _():
        st_ref[...] = jnp.zeros_like(st_ref)

    scale = GLA_DK ** -0.5
    nt = (((1,), (1,)), ((), ()))
    tn = (((0,), (0,)), ((), ()))

    z = jnp.dot(a_ref[0].astype(BF16), u_ref[0], preferred_element_type=F32) + b_ref[0]
    la = _log_sigmoid(z) * (1.0 / GLA_TAU)

    la_hi = la.astype(BF16)
    la_lo = (la - la_hi.astype(F32)).astype(BF16)
    cum2 = jnp.dot(tri_ref[...], jnp.concatenate([la_hi, la_lo], axis=1), preferred_element_type=F32)
    cum = cum2[:, :GLA_DK] + cum2[:, GLA_DK:]
    lasts = [cum[ch * c:ch * c + 1] if reverse else cum[(ch + 1) * c - 1:(ch + 1) * c] for ch in range(nch)]
    last_rows = jnp.concatenate([jnp.broadcast_to(l, (c, GLA_DK)) for l in lasts], axis=0)

    q = q_ref[0].astype(F32)
    k = k_ref[0].astype(F32)
    v = v_ref[0]
    q_in = ((q * scale) * jnp.exp(cum)).astype(BF16)
    k_in = (k * jnp.exp(-cum)).astype(BF16)
    k_end = (k * jnp.exp(last_rows - cum)).astype(BF16)
    sc = lax.dot_general(q_in, k_in, nt, preferred_element_type=F32)
    sc = jnp.where(msk_ref[...] > 0.5, sc, 0.0).astype(BF16)
    o_intra = jnp.dot(sc, v, preferred_element_type=F32)
    kvs = [lax.dot_general(v[ch * c:(ch + 1) * c], k_end[ch * c:(ch + 1) * c], tn,
                           preferred_element_type=F32) for ch in range(nch)]

    st = st_ref[...]
    order = range(nch - 1, -1, -1) if reverse else range(nch)
    for ch in order:
        sl = slice(ch * c, (ch + 1) * c)
        o = o_intra[sl] + lax.dot_general(q_in[sl], st.astype(BF16), nt, preferred_element_type=F32)
        st = st * jnp.exp(lasts[ch]) + kvs[ch]
        if finalize:
            o = o + of_ref[0, sl, :]
            o = _rms_rows(o, gain_ref[...])
            g = g_ref[0, sl, :].astype(F32)
            o_ref[0, sl, :] = (o * (g * jax.nn.sigmoid(g))).astype(o_ref.dtype)
        else:
            o_ref[0, sl, :] = o
    st_ref[...] = st


def gla_direction(proj, decay_in, u, bias, *, reverse, o_fwd=None, out_gain=None):
    b, l, _ = proj.shape
    rows = _pick(l, (512, 256, 128, 64))
    nb = l // rows
    finalize = o_fwd is not None
    blk = (lambda s: nb - 1 - s) if reverse else (lambda s: s)
    kq = GLA_QK_W // GLA_DK
    kv = 2 * GLA_QK_W // GLA_DV
    kg = kv + GLA_HEADS
    in_specs = [
        pl.BlockSpec((1, rows, GLA_DK), lambda bi, h, s: (bi, blk(s), h)),
        pl.BlockSpec((1, rows, GLA_DK), lambda bi, h, s: (bi, blk(s), kq + h)),
        pl.BlockSpec((1, rows, GLA_DV), lambda bi, h, s: (bi, blk(s), kv + h)),
        pl.BlockSpec((1, rows, LANES), lambda bi, h, s: (bi, blk(s), 0)),
        pl.BlockSpec((1, LANES, GLA_DK), lambda bi, h, s: (h, 0, 0)),
        pl.BlockSpec((1, 1, GLA_DK), lambda bi, h, s: (h, 0, 0)),
        pl.BlockSpec((rows, rows), lambda bi, h, s: (0, 0)),
        pl.BlockSpec((rows, rows), lambda bi, h, s: (0, 0)),
    ]
    ri = np.arange(rows)[:, None]
    ci = np.arange(rows)[None, :]
    same = (ri // GLA_CHUNK) == (ci // GLA_CHUNK)
    tri = same & ((ci >= ri) if reverse else (ci <= ri))
    msk = same & ((ci > ri) if reverse else (ci <= ri))
    args = [proj, proj, proj, decay_in, u, bias, jnp.asarray(tri, BF16), jnp.asarray(msk, F32)]
    if finalize:
        in_specs += [
            pl.BlockSpec((1, rows, GLA_DV), lambda bi, h, s: (bi, blk(s), h)),
            pl.BlockSpec((1, rows, GLA_DV), lambda bi, h, s: (bi, blk(s), kg + h)),
            pl.BlockSpec((1, GLA_DV), lambda bi, h, s: (0, 0)),
        ]
        args += [o_fwd, proj, out_gain.reshape(1, GLA_DV)]
    out_dtype = BF16 if finalize else F32
    return pl.pallas_call(
        functools.partial(_gla_kernel, reverse=reverse, finalize=finalize, rows=rows),
        grid=(b, GLA_HEADS, nb),
        in_specs=in_specs,
        out_specs=pl.BlockSpec((1, rows, GLA_DV), lambda bi, h, s: (bi, blk(s), h)),
        out_shape=jax.ShapeDtypeStruct((b, l, GLA_V_W), out_dtype),
        scratch_shapes=[pltpu.VMEM((GLA_DV, GLA_DK), F32)],
        compiler_params=_cparams(("parallel", "parallel", "arbitrary")),
        name="gla_bwd" if reverse else "gla_fwd",
    )(*args)


def _t5_buckets(rel):
    half = N_BUCKETS // 2
    ret = (rel > 0).astype(np.int32) * half
    n = np.abs(rel)
    max_exact = half // 2
    large = max_exact + (np.log(np.maximum(n, 1) / max_exact) / np.log(MAX_DISTANCE / max_exact)
                         * (half - max_exact)).astype(np.int32)
    large = np.minimum(large, half - 1)
    return ret + np.where(n < max_exact, n, large)


def _bias_table_kernel(rb_ref, bkt_ref, o_ref):
    h = pl.program_id(0)
    bkt = bkt_ref[...]
    for g in range(SWA_GROUP):
        acc = jnp.zeros(bkt.shape, F32)
        for b in range(N_BUCKETS):
            acc = jnp.where(bkt == b, rb_ref[b, h * SWA_GROUP + g], acc)
        o_ref[0, g * WINDOW:(g + 1) * WINDOW, :] = acc


def swa_bias_table(rel_bias):
    rel = np.arange(3 * WINDOW)[None, :] - WINDOW - np.arange(WINDOW)[:, None]
    bkt = jnp.asarray(_t5_buckets(rel), jnp.int32)
    return pl.pallas_call(
        _bias_table_kernel,
        grid=(SWA_HKV,),
        in_specs=[
            pl.BlockSpec(memory_space=pltpu.SMEM),
            pl.BlockSpec((WINDOW, 3 * WINDOW), lambda h: (0, 0)),
        ],
        out_specs=pl.BlockSpec((1, SWA_GROUP * WINDOW, 3 * WINDOW), lambda h: (h, 0, 0)),
        out_shape=jax.ShapeDtypeStruct((SWA_HKV, SWA_GROUP * WINDOW, 3 * WINDOW), F32),
        compiler_params=_cparams(("parallel",)),
        name="swa_bias_table",
    )(rel_bias.astype(F32), bkt)


def _swa_kernel(sink_ref, q_ref, kp_ref, kc_ref, kn_ref, vp_ref, vc_ref, vn_ref, bias_ref,
                qg_ref, kg_ref, o_ref, *, nb):
    i = pl.program_id(1)
    w = WINDOW
    gw = SWA_GROUP * HEAD_DIM
    row = lax.broadcasted_iota(jnp.int32, (w, 3 * w), 0)
    col = lax.broadcasted_iota(jnp.int32, (w, 3 * w), 1)
    rel = col - w - row
    valid = (jnp.abs(rel) <= w) & ((col >= w) | (i > 0)) & ((col < 2 * w) | (i < nb - 1))
    heads = [(h, g) for h in range(SWA_HKV) for g in range(SWA_GROUP)]
    kns, vs = [], []
    for h in range(SWA_HKV):
        hs = slice(h * HEAD_DIM, (h + 1) * HEAD_DIM)
        k = jnp.concatenate([kp_ref[0, :, hs], kc_ref[0, :, hs], kn_ref[0, :, hs]], axis=0).astype(F32)
        kns.append(_rms_rows(k, kg_ref[...]).astype(BF16))
        vs.append(jnp.concatenate([vp_ref[0, :, hs], vc_ref[0, :, hs], vn_ref[0, :, hs]], axis=0))
    scores = []
    for h, g in heads:
        qs = slice(h * gw + g * HEAD_DIM, h * gw + (g + 1) * HEAD_DIM)
        qn = _rms_rows(q_ref[0, :, qs].astype(F32), qg_ref[...]).astype(BF16)
        scores.append(lax.dot_general(qn, kns[h], (((1,), (1,)), ((), ())), preferred_element_type=F32))
    probs = []
    for (h, g), s in zip(heads, scores):
        logits = s * (HEAD_DIM ** -0.5) + bias_ref[h, g * w:(g + 1) * w, :]
        logits = jnp.where(valid, logits, NEG_INF)
        sink = sink_ref[h * SWA_GROUP + g]
        m = jnp.maximum(jnp.max(logits, axis=-1, keepdims=True), sink)
        p = jnp.exp(logits - m)
        denom = jnp.sum(p, axis=-1, keepdims=True) + jnp.exp(sink - m)
        probs.append((p.astype(BF16), denom))
    for (h, g), (p, denom) in zip(heads, probs):
        qs = slice(h * gw + g * HEAD_DIM, h * gw + (g + 1) * HEAD_DIM)
        o = jnp.dot(p, vs[h], preferred_element_type=F32) / denom
        o_ref[0, :, qs] = o.astype(o_ref.dtype)


def swa_mixer(proj, bias_tab, sink, q_gain, k_gain):
    b, l, _ = proj.shape
    nb = l // WINDOW
    q0 = OFF_GATE_END // SWA_Q_W
    k0 = (OFF_GATE_END + SWA_Q_W) // SWA_KV_W
    v0 = k0 + 1
    assert OFF_GATE_END % SWA_Q_W == 0 and (OFF_GATE_END + SWA_Q_W) % SWA_KV_W == 0

    def kv_spec(base, shift):
        return pl.BlockSpec(
            (1, WINDOW, SWA_KV_W),
            lambda bi, i: (bi, jnp.clip(i + shift, 0, nb - 1), base))

    return pl.pallas_call(
        functools.partial(_swa_kernel, nb=nb),
        grid=(b, nb),
        in_specs=[
            pl.BlockSpec(memory_space=pltpu.SMEM),
            pl.BlockSpec((1, WINDOW, SWA_Q_W), lambda bi, i: (bi, i, q0)),
            kv_spec(k0, -1), kv_spec(k0, 0), kv_spec(k0, 1),
            kv_spec(v0, -1), kv_spec(v0, 0), kv_spec(v0, 1),
            pl.BlockSpec((SWA_HKV, SWA_GROUP * WINDOW, 3 * WINDOW), lambda bi, i: (0, 0, 0)),
            pl.BlockSpec((1, HEAD_DIM), lambda bi, i: (0, 0)),
            pl.BlockSpec((1, HEAD_DIM), lambda bi, i: (0, 0)),
        ],
        out_specs=pl.BlockSpec((1, WINDOW, SWA_Q_W), lambda bi, i: (bi, i, 0)),
        out_shape=jax.ShapeDtypeStruct((b, l, SWA_Q_W), BF16),
        compiler_params=_cparams(("parallel", "arbitrary")),
        name="swa",
    )(sink.astype(F32), proj, proj, proj, proj, proj, proj, proj, bias_tab,
      q_gain.reshape(1, HEAD_DIM), k_gain.reshape(1, HEAD_DIM))


def _out_proj_kernel(a_ref, b_ref, wa_ref, wb_ref, x_ref, o_ref):
    o_ref[...] = (x_ref[...]
                  + jnp.dot(a_ref[...], wa_ref[...], preferred_element_type=F32)
                  + jnp.dot(b_ref[...], wb_ref[...], preferred_element_type=F32))


def out_proj(o_gla, o_swa, w_out, x):
    t, d = x.shape
    ka, kb = o_gla.shape[1], o_swa.shape[1]
    bm = _pick(t, (1024, 512, 256, 128, 64, 32, 16, 8))
    bn = _pick(d, (1024, 512, 256, 128))
    assert ka == kb and w_out.shape[0] == ka + kb
    return pl.pallas_call(
        _out_proj_kernel,
        grid=(t // bm, d // bn),
        in_specs=[
            pl.BlockSpec((bm, ka), lambda i, j: (i, 0)),
            pl.BlockSpec((bm, kb), lambda i, j: (i, 0)),
            pl.BlockSpec((ka, bn), lambda i, j: (0, j)),
            pl.BlockSpec((kb, bn), lambda i, j: (1, j)),
            pl.BlockSpec((bm, bn), lambda i, j: (i, j)),
        ],
        out_specs=pl.BlockSpec((bm, bn), lambda i, j: (i, j)),
        out_shape=jax.ShapeDtypeStruct((t, d), F32),
        compiler_params=_cparams(("parallel", "arbitrary")),
        name="out_proj",
    )(o_gla, o_swa, w_out, w_out, x)


def _cross_kv_kernel(m_ref, g_ref, w_ref, kg_ref, k_ref, v_ref, hn_ref):
    _norm_block_to(m_ref.at[0], g_ref, hn_ref)
    kv = jnp.dot(hn_ref[...], w_ref[...], preferred_element_type=F32)
    for h in range(CX_HEADS):
        kh = kv[:, h * HEAD_DIM:(h + 1) * HEAD_DIM]
        k_ref[0, :, h * HEAD_DIM:(h + 1) * HEAD_DIM] = _rms_rows(kh, kg_ref[...]).astype(k_ref.dtype)
    v_ref[0] = kv[:, CX_W:].astype(v_ref.dtype)


def cross_kv(mem, g_m, wkv, k_gain):
    b, n_mem, d = mem.shape
    return pl.pallas_call(
        _cross_kv_kernel,
        grid=(b,),
        in_specs=[
            pl.BlockSpec((1, n_mem, d), lambda i: (i, 0, 0)),
            pl.BlockSpec((1, d), lambda i: (0, 0)),
            pl.BlockSpec((d, 2 * CX_W), lambda i: (0, 0)),
            pl.BlockSpec((1, HEAD_DIM), lambda i: (0, 0)),
        ],
        out_specs=[
            pl.BlockSpec((1, n_mem, CX_W), lambda i: (i, 0, 0)),
            pl.BlockSpec((1, n_mem, CX_W), lambda i: (i, 0, 0)),
        ],
        out_shape=[jax.ShapeDtypeStruct((b, n_mem, CX_W), BF16)] * 2,
        scratch_shapes=[pltpu.VMEM((n_mem, d), BF16)],
        compiler_params=_cparams(("parallel",)),
        name="cross_kv",
    )(mem, g_m.reshape(1, d), wkv, k_gain.reshape(1, HEAD_DIM))


def _cross_kernel(x_ref, g_ref, wq_ref, qg_ref, k_ref, v_ref, wo_ref, o_ref, hn_ref):
    _norm_block_to(x_ref.at[0], g_ref, hn_ref)
    q = jnp.dot(hn_ref[...], wq_ref[...], preferred_element_type=F32)
    hsl = [slice(h * HEAD_DIM, (h + 1) * HEAD_DIM) for h in range(CX_HEADS)]
    qhs = [_rms_rows(q[:, hs], qg_ref[...]).astype(BF16) for hs in hsl]
    ss = [lax.dot_general(qh, k_ref[0, :, hs], (((1,), (1,)), ((), ())), preferred_element_type=F32)
          for qh, hs in zip(qhs, hsl)]
    ps = []
    for s in ss:
        s = s * (HEAD_DIM ** -0.5)
        m = jnp.max(s, axis=-1, keepdims=True)
        p = jnp.exp(s - m)
        ps.append((p.astype(BF16), jnp.sum(p, axis=-1, keepdims=True)))
    outs = [jnp.dot(p, v_ref[0, :, hs], preferred_element_type=F32) / denom
            for (p, denom), hs in zip(ps, hsl)]
    o = jnp.concatenate(outs, axis=1).astype(BF16)
    o_ref[0] = x_ref[0] + jnp.dot(o, wo_ref[...], preferred_element_type=F32)


def cross_attention(x, kmem, vmem, g_x, wq, q_gain, wo):
    b, l, d = x.shape
    n_mem = kmem.shape[1]
    bm = _pick(l, (512, 256, 128, 64, 32, 16, 8))
    return pl.pallas_call(
        _cross_kernel,
        grid=(b, l // bm),
        in_specs=[
            pl.BlockSpec((1, bm, d), lambda bi, i: (bi, i, 0)),
            pl.BlockSpec((1, d), lambda bi, i: (0, 0)),
            pl.BlockSpec((d, CX_W), lambda bi, i: (0, 0)),
            pl.BlockSpec((1, HEAD_DIM), lambda bi, i: (0, 0)),
            pl.BlockSpec((1, n_mem, CX_W), lambda bi, i: (bi, 0, 0)),
            pl.BlockSpec((1, n_mem, CX_W), lambda bi, i: (bi, 0, 0)),
            pl.BlockSpec((CX_W, d), lambda bi, i: (0, 0)),
        ],
        out_specs=pl.BlockSpec((1, bm, d), lambda bi, i: (bi, i, 0)),
        out_shape=jax.ShapeDtypeStruct((b, l, d), F32),
        scratch_shapes=[pltpu.VMEM((bm, d), BF16)],
        compiler_params=_cparams(("parallel", "arbitrary")),
        name="cross_attention",
    )(x, g_x.reshape(1, d), wq, q_gain.reshape(1, HEAD_DIM), kmem, vmem, wo)


def _pack_bf16_pairs(x):
    half = x.shape[1] // 2
    bits = lax.bitcast_convert_type(x.astype(BF16).astype(F32), jnp.uint32)
    return (bits[:, :half] & jnp.uint32(0xFFFF0000)) | (bits[:, half:] >> 16)


def _unpack_bf16_pairs(w):
    hi = lax.bitcast_convert_type(w & jnp.uint32(0xFFFF0000), F32)
    lo = lax.bitcast_convert_type(w << 16, F32)
    return hi, lo


def _router_kernel(x_ref, g_ref, rhi_ref, rlo_ref, idx_ref, w_ref, hp_ref, *, n_experts):
    hn = _rms_rows(x_ref[...], g_ref[...])
    hp_ref[...] = _pack_bf16_pairs(hn)
    hi = hn.astype(BF16)
    lo = (hn - hi.astype(F32)).astype(BF16)
    logits = (jnp.dot(hi, rhi_ref[...], preferred_element_type=F32)
              + jnp.dot(hi, rlo_ref[...], preferred_element_type=F32)
              + jnp.dot(lo, rhi_ref[...], preferred_element_type=F32))
    lane = lax.broadcasted_iota(jnp.int32, logits.shape, 1).astype(F32)
    low = jnp.float32(-3.0e38)
    l1 = jnp.where(lane < n_experts, logits, low)
    m1 = jnp.max(l1, axis=-1, keepdims=True)
    i1 = jnp.min(jnp.where(l1 == m1, lane, float(LANES)), axis=-1, keepdims=True)
    l2 = jnp.where(lane == i1, low, l1)
    m2 = jnp.max(l2, axis=-1, keepdims=True)
    i2 = jnp.min(jnp.where(l2 == m2, lane, float(LANES)), axis=-1, keepdims=True)
    e2 = jnp.exp(m2 - m1)
    w1 = 1.0 / (1.0 + e2)
    w2 = e2 / (1.0 + e2)
    idx_ref[...] = jnp.where(lane == 0.0, i1, jnp.where(lane == 1.0, i2, 0.0)).astype(jnp.int32)
    w_ref[...] = jnp.where(lane == 0.0, w1, jnp.where(lane == 1.0, w2, 0.0))


def route_tokens(x, gain, router):
    t, d = x.shape
    n_experts = router.shape[1]
    r = jnp.zeros((d, LANES), F32).at[:, :n_experts].set(router.astype(F32))
    r_hi = r.astype(BF16)
    r_lo = (r - r_hi.astype(F32)).astype(BF16)
    bm = _pick(t, (256, 128, 64, 32, 16, 8))
    return pl.pallas_call(
        functools.partial(_router_kernel, n_experts=n_experts),
        grid=(t // bm,),
        in_specs=[
            pl.BlockSpec((bm, d), lambda i: (i, 0)),
            pl.BlockSpec((1, d), lambda i: (0, 0)),
            pl.BlockSpec((d, LANES), lambda i: (0, 0)),
            pl.BlockSpec((d, LANES), lambda i: (0, 0)),
        ],
        out_specs=[pl.BlockSpec((bm, LANES), lambda i: (i, 0)), pl.BlockSpec((bm, LANES), lambda i: (i, 0)),
                   pl.BlockSpec((bm, d // 2), lambda i: (i, 0))],
        out_shape=[jax.ShapeDtypeStruct((t, LANES), jnp.int32), jax.ShapeDtypeStruct((t, LANES), F32),
                   jax.ShapeDtypeStruct((t, d // 2), jnp.uint32)],
        compiler_params=_cparams(("parallel",)),
        name="moe_router",
    )(x, gain.reshape(1, d), r_hi, r_lo)


def _swiglu_step(hn, wg_ref, wu_ref, wd_ref):
    a = jnp.dot(hn, wg_ref[0], preferred_element_type=F32)
    b = jnp.dot(hn, wu_ref[0], preferred_element_type=F32)
    hid = (a * jax.nn.sigmoid(a)) * b
    return jnp.dot(hid.astype(BF16), wd_ref[0], preferred_element_type=F32)


def _ffn_kernel(x_ref, g_ref, wg_ref, wu_ref, wd_ref, o_ref, hn_ref):
    @pl.when(pl.program_id(1) == 0)
    def _():
        _norm_block_to(x_ref, g_ref, hn_ref)
        o_ref[...] = x_ref[...]

    o_ref[...] += _swiglu_step(hn_ref[...], wg_ref, wu_ref, wd_ref)


def swiglu_ffn(x, gain, w_gu, w_down):
    t, d = x.shape
    ffn = w_down.shape[1]
    bm = _pick(t, (512, 256, 128, 64, 32, 16, 8))
    tf = _pick(ffn, (256, 128))
    nf = ffn // tf
    return pl.pallas_call(
        _ffn_kernel,
        grid=(t // bm, nf),
        in_specs=[
            pl.BlockSpec((bm, d), lambda i, f: (i, 0)),
            pl.BlockSpec((1, d), lambda i, f: (0, 0)),
            pl.BlockSpec((1, d, tf), lambda i, f: (0, 0, f)),
            pl.BlockSpec((1, d, tf), lambda i, f: (0, 0, nf + f)),
            pl.BlockSpec((1, tf, d), lambda i, f: (0, f, 0)),
        ],
        out_specs=pl.BlockSpec((bm, d), lambda i, f: (i, 0)),
        out_shape=jax.ShapeDtypeStruct((t, d), F32),
        scratch_shapes=[pltpu.VMEM((bm, d), BF16)],
        compiler_params=_cparams(("parallel", "arbitrary")),
        name="dense_ffn",
    )(x, gain.reshape(1, d), w_gu, w_gu, w_down)


MOE_TILE_ROWS = 512
MOE_MOVE_TOKENS = 256
MOE_ZERO_ROWS = 128


def _moe_plan(top_idx, n_e, bm, n_tiles):
    e_flat = top_idx[:, :TOP_K].reshape(-1)
    onehot = (e_flat[:, None] == jnp.arange(n_e, dtype=jnp.int32)[None, :]).astype(jnp.int32)
    csum = jnp.cumsum(onehot, axis=0)
    cnt = csum[-1]
    rank = jnp.sum(onehot * (csum - 1), axis=1)
    padded = ((cnt + bm - 1) // bm) * bm
    gend = jnp.cumsum(padded)
    gstart = gend - padded
    dest = (jnp.sum(onehot * gstart[None, :], axis=1) + rank).astype(jnp.int32)
    n_valid = (gend[-1] // bm).astype(jnp.int32)
    tile_start = jnp.arange(n_tiles, dtype=jnp.int32) * bm
    tile_e = jnp.sum((tile_start[:, None] >= gend[None, :]).astype(jnp.int32), axis=1)
    tile_e = jnp.minimum(tile_e, n_e - 1)
    last_e = jnp.sum(jnp.where(jnp.arange(n_tiles) == n_valid - 1, tile_e, 0))
    tile_e = jnp.where(jnp.arange(n_tiles) < n_valid, tile_e, last_e).astype(jnp.int32)
    return dest, tile_e, n_valid.reshape(1), gend.astype(jnp.int32), cnt.astype(jnp.int32)


def _dispatch_kernel(gend_ref, cnt_ref, nv_ref, dest_ref, x_ref, xs_hbm, zbuf, sem, zsem, *,
                     bt, bm, n_e, n_tiles):
    i = pl.program_id(0)

    @pl.when(i == 0)
    def _():
        zbuf[...] = jnp.zeros_like(zbuf)
        zr = zbuf.shape[0]

        def clear_tile(base):
            copies = [pltpu.make_async_copy(
                zbuf, xs_hbm.at[pl.ds(pl.multiple_of(base + z * zr, zr), zr)], zsem)
                for z in range(bm // zr)]
            for cp in copies:
                cp.start()
            for cp in copies:
                cp.wait()

        for e in range(n_e):
            @pl.when(cnt_ref[e] > 0)
            def _():
                clear_tile(gend_ref[e] - bm)

        for back in range(1, min(n_e, n_tiles) + 1):
            @pl.when(n_tiles - back >= nv_ref[0])
            def _():
                clear_tile((n_tiles - back) * bm)

    def issue(r, carry):
        for k in range(TOP_K):
            pltpu.make_async_copy(x_ref.at[pl.ds(r, 1)],
                                  xs_hbm.at[pl.ds(dest_ref[0, 0, TOP_K * r + k], 1)], sem).start()
        return carry

    lax.fori_loop(0, bt, issue, 0)
    for k in range(TOP_K):
        pltpu.make_async_copy(x_ref, xs_hbm.at[pl.ds(0, bt)], sem).wait()


def moe_dispatch(x, dest, gend, cnt, n_valid, n_tiles, bm):
    t, d = x.shape
    n_rows = n_tiles * bm
    bt = _pick(t, (MOE_MOVE_TOKENS, 64, 32, 16, 8))
    n_steps = t // bt
    zr = min(MOE_ZERO_ROWS, bm)
    return pl.pallas_call(
        functools.partial(_dispatch_kernel, bt=bt, bm=bm, n_e=cnt.shape[0], n_tiles=n_tiles),
        grid_spec=pltpu.PrefetchScalarGridSpec(
            num_scalar_prefetch=3,
            grid=(n_steps,),
            in_specs=[
                pl.BlockSpec((1, 1, TOP_K * bt), lambda i, ge, cn, nv: (i, 0, 0), memory_space=pltpu.SMEM),
                pl.BlockSpec((bt, d), lambda i, ge, cn, nv: (i, 0)),
            ],
            out_specs=pl.BlockSpec(memory_space=pl.ANY),
            scratch_shapes=[pltpu.VMEM((zr, d), x.dtype), pltpu.SemaphoreType.DMA(()),
                            pltpu.SemaphoreType.DMA(())],
        ),
        out_shape=jax.ShapeDtypeStruct((n_rows, d), x.dtype),
        compiler_params=_cparams(("arbitrary",)),
        name="moe_dispatch",
    )(gend, cnt, n_valid, dest.reshape(n_steps, 1, TOP_K * bt), x)


def _moe_ffn_kernel(te_ref, nv_ref, x_ref, wg_ref, wu_ref, wd_ref, o_ref, hn_ref, acc_ref, *, nf, slab):
    j = pl.program_id(0)
    f = pl.program_id(1)
    valid = j < nv_ref[0]
    rows, half = x_ref.shape

    @pl.when(valid & (f == 0))
    def _():
        def unpack(r, carry):
            r0 = pl.multiple_of(r * slab, slab)
            hi, lo = _unpack_bf16_pairs(x_ref[pl.ds(r0, slab), :])
            hn_ref[pl.ds(r0, slab), pl.ds(0, half)] = hi.astype(hn_ref.dtype)
            hn_ref[pl.ds(r0, slab), pl.ds(half, half)] = lo.astype(hn_ref.dtype)
            return carry

        lax.fori_loop(0, rows // slab, unpack, 0)
        acc_ref[...] = _swiglu_step(hn_ref[...], wg_ref, wu_ref, wd_ref)

    @pl.when(valid & (f > 0))
    def _():
        acc_ref[...] += _swiglu_step(hn_ref[...], wg_ref, wu_ref, wd_ref)

    @pl.when(valid & (f == nf - 1))
    def _():
        def pack(r, carry):
            r0 = pl.multiple_of(r * slab, slab)
            o_ref[pl.ds(r0, slab), :] = _pack_bf16_pairs(acc_ref[pl.ds(r0, slab), :])
            return carry

        lax.fori_loop(0, rows // slab, pack, 0)

    @pl.when(jnp.logical_not(valid) & (f == nf - 1))
    def _():
        o_ref[...] = jnp.zeros_like(o_ref)


def moe_expert_ffn(xs, w_gu, w_down, tile_e, n_valid, bm):
    n_rows, half = xs.shape
    d = 2 * half
    n_tiles = n_rows // bm
    slab = min(64, bm)
    ffn = w_down.shape[1]
    tf = _pick(ffn, (256, 128))
    nf = ffn // tf

    def fidx(j, f, nv):
        return jnp.where(j < nv[0], f, nf - 1)

    return pl.pallas_call(
        functools.partial(_moe_ffn_kernel, nf=nf, slab=slab),
        grid_spec=pltpu.PrefetchScalarGridSpec(
            num_scalar_prefetch=2,
            grid=(n_tiles, nf),
            in_specs=[
                pl.BlockSpec((bm, half), lambda j, f, te, nv: (jnp.minimum(j, nv[0] - 1), 0)),
                pl.BlockSpec((1, d, tf), lambda j, f, te, nv: (te[j], 0, fidx(j, f, nv))),
                pl.BlockSpec((1, d, tf), lambda j, f, te, nv: (te[j], 0, nf + fidx(j, f, nv))),
                pl.BlockSpec((1, tf, d), lambda j, f, te, nv: (te[j], fidx(j, f, nv), 0)),
            ],
            out_specs=pl.BlockSpec((bm, half), lambda j, f, te, nv: (j, 0)),
            scratch_shapes=[pltpu.VMEM((bm, d), BF16), pltpu.VMEM((bm, d), F32)],
        ),
        out_shape=jax.ShapeDtypeStruct((n_rows, half), jnp.uint32),
        compiler_params=_cparams(("arbitrary", "arbitrary")),
        name="moe_ffn",
    )(tile_e, n_valid, xs, w_gu, w_gu, w_down)


def _combine_kernel(pos_ref, posn_ref, x_ref, w_ref, y_hbm, o_ref, ybuf, sems, *, bt, n_steps):
    i = pl.program_id(0)
    slot = i % 2

    def issue(p_ref, s):
        def body(r, carry):
            for k in range(TOP_K):
                pltpu.make_async_copy(y_hbm.at[pl.ds(p_ref[0, 0, TOP_K * r + k], 1)],
                                      ybuf.at[s, k, pl.ds(r, 1)], sems.at[s]).start()
            return carry

        lax.fori_loop(0, bt, body, 0)

    @pl.when(i == 0)
    def _():
        issue(pos_ref, 0)

    @pl.when(i + 1 < n_steps)
    def _():
        issue(posn_ref, 1 - slot)

    for k in range(TOP_K):
        pltpu.make_async_copy(y_hbm.at[pl.ds(0, bt)], ybuf.at[slot, k], sems.at[slot]).wait()
    w = w_ref[...]
    half = ybuf.shape[-1]
    a_hi, a_lo = _unpack_bf16_pairs(ybuf[slot, 0])
    b_hi, b_lo = _unpack_bf16_pairs(ybuf[slot, 1])
    o_ref[:, :half] = x_ref[:, :half] + w[:, 0:1] * a_hi + w[:, 1:2] * b_hi
    o_ref[:, half:] = x_ref[:, half:] + w[:, 0:1] * a_lo + w[:, 1:2] * b_lo


def moe_combine(x, top_w, dest, ys):
    t, d = x.shape
    bt = _pick(t, (MOE_MOVE_TOKENS, 64, 32, 16, 8))
    n_steps = t // bt
    pos = dest.reshape(n_steps, 1, TOP_K * bt)
    return pl.pallas_call(
        functools.partial(_combine_kernel, bt=bt, n_steps=n_steps),
        grid=(n_steps,),
        in_specs=[
            pl.BlockSpec((1, 1, TOP_K * bt), lambda i: (i, 0, 0), memory_space=pltpu.SMEM),
            pl.BlockSpec((1, 1, TOP_K * bt), lambda i: (jnp.minimum(i + 1, n_steps - 1), 0, 0),
                         memory_space=pltpu.SMEM),
            pl.BlockSpec((bt, d), lambda i: (i, 0)),
            pl.BlockSpec((bt, LANES), lambda i: (i, 0)),
            pl.BlockSpec(memory_space=pl.ANY),
        ],
        out_specs=pl.BlockSpec((bt, d), lambda i: (i, 0)),
        out_shape=jax.ShapeDtypeStruct((t, d), F32),
        scratch_shapes=[pltpu.VMEM((2, TOP_K, bt, d // 2), jnp.uint32), pltpu.SemaphoreType.DMA((2,))],
        compiler_params=_cparams(("arbitrary",)),
        name="moe_combine",
    )(pos, pos, x, top_w, ys)


def moe_ffn(x, gain, router, w_gu, w_down):
    t, d = x.shape
    n_e = w_down.shape[0]
    bm = min(MOE_TILE_ROWS, max(8, t // 8))
    n_tiles = (TOP_K * t + n_e * (bm - 1) + bm - 1) // bm
    top_idx, top_w, hn_packed = route_tokens(x, gain, router)
    dest, tile_e, n_valid, gend, cnt = _moe_plan(top_idx, n_e, bm, n_tiles)
    xs = moe_dispatch(hn_packed, dest, gend, cnt, n_valid, n_tiles, bm)
    ys = moe_expert_ffn(xs, w_gu, w_down, tile_e, n_valid, bm)
    return moe_combine(x, top_w, dest, ys)


def _prep_layer(i, p):
    w_in = p["w_in"][i]
    d = w_in.shape[0]
    w_main = jnp.concatenate([w_in[:, :OFF_GATE_END], w_in[:, OFF_DECAY_END:]], axis=1).astype(BF16)
    w_decay = jnp.zeros((d, LANES), BF16).at[:, :2 * GLA_RANK].set(
        w_in[:, OFF_GATE_END:OFF_DECAY_END].astype(BF16))

    def up_mat(up, row0):
        u = up.reshape(GLA_RANK, GLA_HEADS, GLA_DK).transpose(1, 0, 2).astype(BF16)
        return jnp.zeros((GLA_HEADS, LANES, GLA_DK), BF16).at[:, row0:row0 + GLA_RANK, :].set(u)

    return dict(
        norm_mix=p["norm_mix"][i], w_main=w_main, w_decay=w_decay,
        u_f=up_mat(p["gla_up_f"][i], 0), u_b=up_mat(p["gla_up_b"][i], GLA_RANK),
        bias_f=p["gla_bias_f"][i].reshape(GLA_HEADS, 1, GLA_DK).astype(F32),
        bias_b=p["gla_bias_b"][i].reshape(GLA_HEADS, 1, GLA_DK).astype(F32),
        gla_out_norm=p["gla_out_norm"][i], swa_q_norm=p["swa_q_norm"][i], swa_k_norm=p["swa_k_norm"][i],
        swa_sink=p["swa_sink"][i], w_out=p["w_out"][i].astype(BF16),
        norm_cross=p["norm_cross"][i], norm_mem=p["norm_mem"][i],
        cx_wq=p["cx_wq"][i].astype(BF16), cx_wkv=p["cx_wkv"][i].astype(BF16),
        cx_q_norm=p["cx_q_norm"][i], cx_k_norm=p["cx_k_norm"][i], cx_wo=p["cx_wo"][i].astype(BF16),
        norm_ffn=p["norm_ffn"][i],
    )


def _trunk(x, mem, layers, bias_tab):
    b, l, d = x.shape
    t = b * l
    for i, lw in enumerate(layers):
        proj, decay_in = in_proj(x.reshape(t, d), lw["norm_mix"], lw["w_main"], lw["w_decay"])
        proj = proj.reshape(b, l, MAIN_W)
        decay_in = decay_in.reshape(b, l, LANES)
        o_f = gla_direction(proj, decay_in, lw["u_f"], lw["bias_f"], reverse=False)
        o_gla = gla_direction(proj, decay_in, lw["u_b"], lw["bias_b"], reverse=True,
                              o_fwd=o_f, out_gain=lw["gla_out_norm"])
        o_swa = swa_mixer(proj, bias_tab, lw["swa_sink"], lw["swa_q_norm"], lw["swa_k_norm"])
        x1 = out_proj(o_gla.reshape(t, GLA_V_W), o_swa.reshape(t, SWA_Q_W), lw["w_out"], x.reshape(t, d))
        kmem, vmem = cross_kv(mem, lw["norm_mem"], lw["cx_wkv"], lw["cx_k_norm"])
        x2 = cross_attention(x1.reshape(b, l, d), kmem, vmem, lw["norm_cross"], lw["cx_wq"],
                             lw["cx_q_norm"], lw["cx_wo"]).reshape(t, d)
        if i % 2 == 0:
            x3 = swiglu_ffn(x2, lw["norm_ffn"], lw["ffn_w_gu"], lw["ffn_w_down"])
        else:
            x3 = moe_ffn(x2, lw["norm_ffn"], lw["moe_router"], lw["moe_w_gu"], lw["moe_w_down"])
        x = x3.reshape(b, l, d)
    return x


def kernel(x_prompt, x_sample, mem_prompt, mem_sample, rel_bias, norm_mix, w_in, gla_up_f, gla_bias_f, gla_up_b, gla_bias_b, gla_out_norm, swa_q_norm, swa_k_norm, swa_sink, w_out, norm_cross, norm_mem, cx_wq, cx_wkv, cx_q_norm, cx_k_norm, cx_wo, norm_ffn, ffn_w_gu, ffn_w_down, moe_router, moe_w_gu, moe_w_down):
    p = dict(norm_mix=norm_mix, w_in=w_in, gla_up_f=gla_up_f, gla_bias_f=gla_bias_f, gla_up_b=gla_up_b,
             gla_bias_b=gla_bias_b, gla_out_norm=gla_out_norm, swa_q_norm=swa_q_norm, swa_k_norm=swa_k_norm,
             swa_sink=swa_sink, w_out=w_out, norm_cross=norm_cross, norm_mem=norm_mem, cx_wq=cx_wq,
             cx_wkv=cx_wkv, cx_q_norm=cx_q_norm, cx_k_norm=cx_k_norm, cx_wo=cx_wo, norm_ffn=norm_ffn)
    depth = w_in.shape[0]
    layers = []
    for i in range(depth):
        lw = _prep_layer(i, p)
        if i % 2 == 0:
            lw["ffn_w_gu"] = ffn_w_gu[i // 2][None].astype(BF16)
            lw["ffn_w_down"] = ffn_w_down[i // 2][None].astype(BF16)
        else:
            lw["moe_router"] = moe_router[i // 2]
            lw["moe_w_gu"] = moe_w_gu[i // 2].astype(BF16)
            lw["moe_w_down"] = moe_w_down[i // 2].astype(BF16)
        layers.append(lw)
    bias_tab = swa_bias_table(rel_bias)
    y_prompt = _trunk(x_prompt, mem_prompt, layers, bias_tab)
    y_sample = _trunk(x_sample, mem_sample, layers, bias_tab)
    return (y_prompt, y_sample)
```

```python
import functools

import numpy as np
import jax
import jax.numpy as jnp
from jax import lax
from jax.experimental import pallas as pl
from jax.experimental.pallas import tpu as pltpu

F32 = jnp.float32
BF16 = jnp.bfloat16

HEAD_DIM = 128
GLA_HEADS = 8
GLA_DK = 128
GLA_DV = 256
GLA_RANK = 16
GLA_TAU = 16.0
GLA_CHUNK = 64
GLA_HEADS_PER_STEP = 4
SWA_HQ = 16
SWA_HKV = 4
SWA_GROUP = SWA_HQ // SWA_HKV
WINDOW = 128
N_BUCKETS = 32
MAX_DISTANCE = 128
CX_HEADS = 4
TOP_K = 2
EPS = 1e-6
NEG_INF = -1e30

GLA_QK_W = GLA_HEADS * GLA_DK
GLA_V_W = GLA_HEADS * GLA_DV
SWA_Q_W = SWA_HQ * HEAD_DIM
SWA_KV_W = SWA_HKV * HEAD_DIM
CX_W = CX_HEADS * HEAD_DIM
OFF_GATE_END = 2 * GLA_QK_W + 2 * GLA_V_W
OFF_DECAY_END = OFF_GATE_END + 2 * GLA_RANK
MAIN_W = OFF_GATE_END + SWA_Q_W + 2 * SWA_KV_W

LANES = 128
VMEM_LIMIT_BYTES = 56 * 1024 * 1024


def _cparams(sem):
    return pltpu.CompilerParams(dimension_semantics=sem, vmem_limit_bytes=VMEM_LIMIT_BYTES)


def _rms_rows(x, gain):
    ms = jnp.mean(x * x, axis=-1, keepdims=True)
    return x * lax.rsqrt(ms + EPS) * gain


def _norm_block_to(x_ref, g_ref, hn_ref, slab=16):
    rows = x_ref.shape[0]
    slab = min(slab, rows)

    def body(r, carry):
        r0 = pl.multiple_of(r * slab, slab)
        hn_ref[pl.ds(r0, slab), :] = _rms_rows(x_ref[pl.ds(r0, slab), :], g_ref[...]).astype(hn_ref.dtype)
        return carry

    lax.fori_loop(0, rows // slab, body, 0, unroll=8)


def _pick(n, cands):
    for c in cands:
        if n % c == 0:
            return c
    return n


def _in_proj_kernel(x_ref, g_ref, w_ref, wa_ref, o_ref, oa_ref, hn_ref):
    @pl.when(pl.program_id(1) == 0)
    def _():
        _norm_block_to(x_ref, g_ref, hn_ref)
        oa_ref[...] = jnp.dot(hn_ref[...], wa_ref[...], preferred_element_type=F32)

    o_ref[...] = jnp.dot(hn_ref[...], w_ref[...], preferred_element_type=F32).astype(o_ref.dtype)


def in_proj(x, gain, w_main, w_decay):
    t, d = x.shape
    n = w_main.shape[1]
    bm = _pick(t, (512, 256, 128, 64, 32, 16, 8))
    bn = _pick(n, (1024, 768, 512, 256, 128))
    return pl.pallas_call(
        _in_proj_kernel,
        grid=(t // bm, n // bn),
        in_specs=[
            pl.BlockSpec((bm, d), lambda i, j: (i, 0)),
            pl.BlockSpec((1, d), lambda i, j: (0, 0)),
            pl.BlockSpec((d, bn), lambda i, j: (0, j)),
            pl.BlockSpec((d, LANES), lambda i, j: (0, 0)),
        ],
        out_specs=[
            pl.BlockSpec((bm, bn), lambda i, j: (i, j)),
            pl.BlockSpec((bm, LANES), lambda i, j: (i, 0)),
        ],
        out_shape=[jax.ShapeDtypeStruct((t, n), BF16), jax.ShapeDtypeStruct((t, LANES), F32)],
        scratch_shapes=[pltpu.VMEM((bm, d), BF16)],
        compiler_params=_cparams(("parallel", "arbitrary")),
        name="in_proj",
    )(x, gain.reshape(1, d), w_main, w_decay)


def _log_sigmoid(z):
    return jnp.minimum(z, 0.0) - jnp.log1p(jnp.exp(-jnp.abs(z)))


def _gla_kernel(*refs, reverse, finalize, rows):
    if finalize:
        q_ref, k_ref, v_ref, a_ref, u_ref, b_ref, tri_ref, msk_ref, of_ref, g_ref, gain_ref, o_ref, st_ref = refs
    else:
        q_ref, k_ref, v_ref, a_ref, u_ref, b_ref, tri_ref, msk_ref, o_ref, st_ref = refs
    c = GLA_CHUNK
    nch = rows // c
    nh = st_ref.shape[0]
    hk = [slice(i * GLA_DK, (i + 1) * GLA_DK) for i in range(nh)]
    hv = [slice(i * GLA_DV, (i + 1) * GLA_DV) for i in range(nh)]

    @pl.when(pl.program_id(2) == 0)
    def _():
        st_ref[...] = jnp.zeros_like(st_ref)

    scale = GLA_DK ** -0.5
    nt = (((1,), (1,)), ((), ()))
    tn = (((0,), (0,)), ((), ()))

    a = a_ref[0].astype(BF16)
    zs = [jnp.dot(a, u_ref[i], preferred_element_type=F32) + b_ref[i] for i in range(nh)]
    las = [_log_sigmoid(z) * (1.0 / GLA_TAU) for z in zs]

    hls = []
    for la in las:
        la_hi = la.astype(BF16)
        la_lo = (la - la_hi.astype(F32)).astype(BF16)
        hls.append(jnp.concatenate([la_hi, la_lo], axis=1))
    cum2s = [jnp.dot(tri_ref[...], hl, preferred_element_type=F32) for hl in hls]

    q_ins, k_ins, k_ends, lasts_all = [], [], [], []
    for i, cum2 in enumerate(cum2s):
        cum = cum2[:, :GLA_DK] + cum2[:, GLA_DK:]
        lasts = [cum[ch * c:ch * c + 1] if reverse else cum[(ch + 1) * c - 1:(ch + 1) * c]
                 for ch in range(nch)]
        last_rows = jnp.concatenate([jnp.broadcast_to(l, (c, GLA_DK)) for l in lasts], axis=0)
        q = q_ref[0, :, hk[i]].astype(F32)
        k = k_ref[0, :, hk[i]].astype(F32)
        q_ins.append(((q * scale) * jnp.exp(cum)).astype(BF16))
        k_ins.append((k * jnp.exp(-cum)).astype(BF16))
        k_ends.append((k * jnp.exp(last_rows - cum)).astype(BF16))
        lasts_all.append(lasts)

    scs = [lax.dot_general(q_in, k_in, nt, preferred_element_type=F32) for q_in, k_in in zip(q_ins, k_ins)]
    scs = [jnp.where(msk_ref[...] > 0.5, sc, 0.0).astype(BF16) for sc in scs]
    o_intras = [jnp.dot(sc, v_ref[0, :, hv[i]], preferred_element_type=F32) for i, sc in enumerate(scs)]
    kvs_all = [[lax.dot_general(v_ref[0, ch * c:(ch + 1) * c, hv[i]], k_ends[i][ch * c:(ch + 1) * c], tn,
                                preferred_element_type=F32) for ch in range(nch)]
               for i in range(nh)]

    order = range(nch - 1, -1, -1) if reverse else range(nch)
    for i in range(nh):
        st = st_ref[i]
        for ch in order:
            sl = slice(ch * c, (ch + 1) * c)
            o = o_intras[i][sl] + lax.dot_general(q_ins[i][sl], st.astype(BF16), nt,
                                                  preferred_element_type=F32)
            st = st * jnp.exp(lasts_all[i][ch]) + kvs_all[i][ch]
            if finalize:
                o = o + of_ref[0, sl, hv[i]]
                o = _rms_rows(o, gain_ref[...])
                g = g_ref[0, sl, hv[i]].astype(F32)
                o_ref[0, sl, hv[i]] = (o * (g * jax.nn.sigmoid(g))).astype(o_ref.dtype)
            else:
                o_ref[0, sl, hv[i]] = o
        st_ref[i] = st


def gla_direction(proj, decay_in, u, bias, *, reverse, o_fwd=None, out_gain=None):
    b, l, _ = proj.shape
    rows = _pick(l, (512, 256, 128, 64))
    nb = l // rows
    finalize = o_fwd is not None
    blk = (lambda s: nb - 1 - s) if reverse else (lambda s: s)
    nh = GLA_HEADS_PER_STEP
    wk, wv = nh * GLA_DK, nh * GLA_DV
    kq = GLA_QK_W // wk
    kv = 2 * GLA_QK_W // wv
    kg = kv + GLA_HEADS // nh
    in_specs = [
        pl.BlockSpec((1, rows, wk), lambda bi, h, s: (bi, blk(s), h)),
        pl.BlockSpec((1, rows, wk), lambda bi, h, s: (bi, blk(s), kq + h)),
        pl.BlockSpec((1, rows, wv), lambda bi, h, s: (bi, blk(s), kv + h)),
        pl.BlockSpec((1, rows, LANES), lambda bi, h, s: (bi, blk(s), 0)),
        pl.BlockSpec((nh, LANES, GLA_DK), lambda bi, h, s: (h, 0, 0)),
        pl.BlockSpec((nh, 1, GLA_DK), lambda bi, h, s: (h, 0, 0)),
        pl.BlockSpec((rows, rows), lambda bi, h, s: (0, 0)),
        pl.BlockSpec((rows, rows), lambda bi, h, s: (0, 0)),
    ]
    ri = np.arange(rows)[:, None]
    ci = np.arange(rows)[None, :]
    same = (ri // GLA_CHUNK) == (ci // GLA_CHUNK)
    tri = same & ((ci >= ri) if reverse else (ci <= ri))
    msk = same & ((ci > ri) if reverse else (ci <= ri))
    args = [proj, proj, proj, decay_in, u, bias, jnp.asarray(tri, BF16), jnp.asarray(msk, F32)]
    if finalize:
        in_specs += [
            pl.BlockSpec((1, rows, wv), lambda bi, h, s: (bi, blk(s), h)),
            pl.BlockSpec((1, rows, wv), lambda bi, h, s: (bi, blk(s), kg + h)),
            pl.BlockSpec((1, GLA_DV), lambda bi, h, s: (0, 0)),
        ]
        args += [o_fwd, proj, out_gain.reshape(1, GLA_DV)]
    out_dtype = BF16 if finalize else F32
    return pl.pallas_call(
        functools.partial(_gla_kernel, reverse=reverse, finalize=finalize, rows=rows),
        grid=(b, GLA_HEADS // nh, nb),
        in_specs=in_specs,
        out_specs=pl.BlockSpec((1, rows, wv), lambda bi, h, s: (bi, blk(s), h)),
        out_shape=jax.ShapeDtypeStruct((b, l, GLA_V_W), out_dtype),
        scratch_shapes=[pltpu.VMEM((nh, GLA_DV, GLA_DK), F32)],
        compiler_params=_cparams(("parallel", "parallel", "arbitrary")),
        name="gla_bwd" if reverse else "gla_fwd",
    )(*args)


def _t5_buckets(rel):
    half = N_BUCKETS // 2
    ret = (rel > 0).astype(np.int32) * half
    n = np.abs(rel)
    max_exact = half // 2
    large = max_exact + (np.log(np.maximum(n, 1) / max_exact) / np.log(MAX_DISTANCE / max_exact)
                         * (half - max_exact)).astype(np.int32)
    large = np.minimum(large, half - 1)
    return ret + np.where(n < max_exact, n, large)


def _bias_table_kernel(rb_ref, bkt_ref, o_ref):
    h = pl.program_id(0)
    bkt = bkt_ref[...]
    for g in range(SWA_GROUP):
        acc = jnp.zeros(bkt.shape, F32)
        for b in range(N_BUCKETS):
            acc = jnp.where(bkt == b, rb_ref[b, h * SWA_GROUP + g], acc)
        o_ref[0, g * WINDOW:(g + 1) * WINDOW, :] = acc


def swa_bias_table(rel_bias):
    rel = np.arange(3 * WINDOW)[None, :] - WINDOW - np.arange(WINDOW)[:, None]
    bkt = jnp.asarray(_t5_buckets(rel), jnp.int32)
    return pl.pallas_call(
        _bias_table_kernel,
        grid=(SWA_HKV,),
        in_specs=[
            pl.BlockSpec(memory_space=pltpu.SMEM),
            pl.BlockSpec((WINDOW, 3 * WINDOW), lambda h: (0, 0)),
        ],
        out_specs=pl.BlockSpec((1, SWA_GROUP * WINDOW, 3 * WINDOW), lambda h: (h, 0, 0)),
        out_shape=jax.ShapeDtypeStruct((SWA_HKV, SWA_GROUP * WINDOW, 3 * WINDOW), F32),
        compiler_params=_cparams(("parallel",)),
        name="swa_bias_table",
    )(rel_bias.astype(F32), bkt)


def _swa_kernel(sink_ref, q_ref, kp_ref, kc_ref, kn_ref, vp_ref, vc_ref, vn_ref, bias_ref,
                qg_ref, kg_ref, o_ref, *, nb):
    i = pl.program_id(1)
    w = WINDOW
    gw = SWA_GROUP * HEAD_DIM
    row = lax.broadcasted_iota(jnp.int32, (w, 3 * w), 0)
    col = lax.broadcasted_iota(jnp.int32, (w, 3 * w), 1)
    rel = col - w - row
    valid = (jnp.abs(rel) <= w) & ((col >= w) | (i > 0)) & ((col < 2 * w) | (i < nb - 1))
    heads = [(h, g) for h in range(SWA_HKV) for g in range(SWA_GROUP)]
    kns, vs = [], []
    for h in range(SWA_HKV):
        hs = slice(h * HEAD_DIM, (h + 1) * HEAD_DIM)
        k = jnp.concatenate([kp_ref[0, :, hs], kc_ref[0, :, hs], kn_ref[0, :, hs]], axis=0).astype(F32)
        kns.append(_rms_rows(k, kg_ref[...]).astype(BF16))
        vs.append(jnp.concatenate([vp_ref[0, :, hs], vc_ref[0, :, hs], vn_ref[0, :, hs]], axis=0))
    scores = []
    for h, g in heads:
        qs = slice(h * gw + g * HEAD_DIM, h * gw + (g + 1) * HEAD_DIM)
        qn = _rms_rows(q_ref[0, :, qs].astype(F32), qg_ref[...]).astype(BF16)
        scores.append(lax.dot_general(qn, kns[h], (((1,), (1,)), ((), ())), preferred_element_type=F32))
    probs = []
    for (h, g), s in zip(heads, scores):
        logits = s * (HEAD_DIM ** -0.5) + bias_ref[h, g * w:(g + 1) * w, :]
        logits = jnp.where(valid, logits, NEG_INF)
        sink = sink_ref[h * SWA_GROUP + g]
        m = jnp.maximum(jnp.max(logits, axis=-1, keepdims=True), sink)
        p = jnp.exp(logits - m)
        denom = jnp.sum(p, axis=-1, keepdims=True) + jnp.exp(sink - m)
        probs.append((p.astype(BF16), denom))
    for (h, g), (p, denom) in zip(heads, probs):
        qs = slice(h * gw + g * HEAD_DIM, h * gw + (g + 1) * HEAD_DIM)
        o = jnp.dot(p, vs[h], preferred_element_type=F32) / denom
        o_ref[0, :, qs] = o.astype(o_ref.dtype)


def swa_mixer(proj, bias_tab, sink, q_gain, k_gain):
    b, l, _ = proj.shape
    nb = l // WINDOW
    q0 = OFF_GATE_END // SWA_Q_W
    k0 = (OFF_GATE_END + SWA_Q_W) // SWA_KV_W
    v0 = k0 + 1
    assert OFF_GATE_END % SWA_Q_W == 0 and (OFF_GATE_END + SWA_Q_W) % SWA_KV_W == 0

    def kv_spec(base, shift):
        return pl.BlockSpec(
            (1, WINDOW, SWA_KV_W),
            lambda bi, i: (bi, jnp.clip(i + shift, 0, nb - 1), base))

    return pl.pallas_call(
        functools.partial(_swa_kernel, nb=nb),
        grid=(b, nb),
        in_specs=[
            pl.BlockSpec(memory_space=pltpu.SMEM),
            pl.BlockSpec((1, WINDOW, SWA_Q_W), lambda bi, i: (bi, i, q0)),
            kv_spec(k0, -1), kv_spec(k0, 0), kv_spec(k0, 1),
            kv_spec(v0, -1), kv_spec(v0, 0), kv_spec(v0, 1),
            pl.BlockSpec((SWA_HKV, SWA_GROUP * WINDOW, 3 * WINDOW), lambda bi, i: (0, 0, 0)),
            pl.BlockSpec((1, HEAD_DIM), lambda bi, i: (0, 0)),
            pl.BlockSpec((1, HEAD_DIM), lambda bi, i: (0, 0)),
        ],
        out_specs=pl.BlockSpec((1, WINDOW, SWA_Q_W), lambda bi, i: (bi, i, 0)),
        out_shape=jax.ShapeDtypeStruct((b, l, SWA_Q_W), BF16),
        compiler_params=_cparams(("parallel", "arbitrary")),
        name="swa",
    )(sink.astype(F32), proj, proj, proj, proj, proj, proj, proj, bias_tab,
      q_gain.reshape(1, HEAD_DIM), k_gain.reshape(1, HEAD_DIM))


def _out_proj_kernel(a_ref, b_ref, wa_ref, wb_ref, x_ref, o_ref):
    o_ref[...] = (x_ref[...]
                  + jnp.dot(a_ref[...], wa_ref[...], preferred_element_type=F32)
                  + jnp.dot(b_ref[...], wb_ref[...], preferred_element_type=F32))


def out_proj(o_gla, o_swa, w_out, x):
    t, d = x.shape
    ka, kb = o_gla.shape[1], o_swa.shape[1]
    bm = _pick(t, (1024, 512, 256, 128, 64, 32, 16, 8))
    bn = _pick(d, (1024, 512, 256, 128))
    assert ka == kb and w_out.shape[0] == ka + kb
    return pl.pallas_call(
        _out_proj_kernel,
        grid=(t // bm, d // bn),
        in_specs=[
            pl.BlockSpec((bm, ka), lambda i, j: (i, 0)),
            pl.BlockSpec((bm, kb), lambda i, j: (i, 0)),
            pl.BlockSpec((ka, bn), lambda i, j: (0, j)),
            pl.BlockSpec((kb, bn), lambda i, j: (1, j)),
            pl.BlockSpec((bm, bn), lambda i, j: (i, j)),
        ],
        out_specs=pl.BlockSpec((bm, bn), lambda i, j: (i, j)),
        out_shape=jax.ShapeDtypeStruct((t, d), F32),
        compiler_params=_cparams(("parallel", "arbitrary")),
        name="out_proj",
    )(o_gla, o_swa, w_out, w_out, x)


def _cross_kv_kernel(m_ref, g_ref, w_ref, kg_ref, k_ref, v_ref, hn_ref):
    _norm_block_to(m_ref.at[0], g_ref, hn_ref)
    kv = jnp.dot(hn_ref[...], w_ref[...], preferred_element_type=F32)
    for h in range(CX_HEADS):
        kh = kv[:, h * HEAD_DIM:(h + 1) * HEAD_DIM]
        k_ref[0, :, h * HEAD_DIM:(h + 1) * HEAD_DIM] = _rms_rows(kh, kg_ref[...]).astype(k_ref.dtype)
    v_ref[0] = kv[:, CX_W:].astype(v_ref.dtype)


def cross_kv(mem, g_m, wkv, k_gain):
    b, n_mem, d = mem.shape
    return pl.pallas_call(
        _cross_kv_kernel,
        grid=(b,),
        in_specs=[
            pl.BlockSpec((1, n_mem, d), lambda i: (i, 0, 0)),
            pl.BlockSpec((1, d), lambda i: (0, 0)),
            pl.BlockSpec((d, 2 * CX_W), lambda i: (0, 0)),
            pl.BlockSpec((1, HEAD_DIM), lambda i: (0, 0)),
        ],
        out_specs=[
            pl.BlockSpec((1, n_mem, CX_W), lambda i: (i, 0, 0)),
            pl.BlockSpec((1, n_mem, CX_W), lambda i: (i, 0, 0)),
        ],
        out_shape=[jax.ShapeDtypeStruct((b, n_mem, CX_W), BF16)] * 2,
        scratch_shapes=[pltpu.VMEM((n_mem, d), BF16)],
        compiler_params=_cparams(("parallel",)),
        name="cross_kv",
    )(mem, g_m.reshape(1, d), wkv, k_gain.reshape(1, HEAD_DIM))


def _cross_kernel(x_ref, g_ref, wq_ref, qg_ref, k_ref, v_ref, wo_ref, o_ref, hn_ref):
    _norm_block_to(x_ref.at[0], g_ref, hn_ref)
    q = jnp.dot(hn_ref[...], wq_ref[...], preferred_element_type=F32)
    hsl = [slice(h * HEAD_DIM, (h + 1) * HEAD_DIM) for h in range(CX_HEADS)]
    qhs = [_rms_rows(q[:, hs], qg_ref[...]).astype(BF16) for hs in hsl]
    ss = [lax.dot_general(qh, k_ref[0, :, hs], (((1,), (1,)), ((), ())), preferred_element_type=F32)
          for qh, hs in zip(qhs, hsl)]
    ps = []
    for s in ss:
        s = s * (HEAD_DIM ** -0.5)
        m = jnp.max(s, axis=-1, keepdims=True)
        p = jnp.exp(s - m)
        ps.append((p.astype(BF16), jnp.sum(p, axis=-1, keepdims=True)))
    outs = [jnp.dot(p, v_ref[0, :, hs], preferred_element_type=F32) / denom
            for (p, denom), hs in zip(ps, hsl)]
    o = jnp.concatenate(outs, axis=1).astype(BF16)
    o_ref[0] = x_ref[0] + jnp.dot(o, wo_ref[...], preferred_element_type=F32)


def cross_attention(x, kmem, vmem, g_x, wq, q_gain, wo):
    b, l, d = x.shape
    n_mem = kmem.shape[1]
    bm = _pick(l, (512, 256, 128, 64, 32, 16, 8))
    return pl.pallas_call(
        _cross_kernel,
        grid=(b, l // bm),
        in_specs=[
            pl.BlockSpec((1, bm, d), lambda bi, i: (bi, i, 0)),
            pl.BlockSpec((1, d), lambda bi, i: (0, 0)),
            pl.BlockSpec((d, CX_W), lambda bi, i: (0, 0)),
            pl.BlockSpec((1, HEAD_DIM), lambda bi, i: (0, 0)),
            pl.BlockSpec((1, n_mem, CX_W), lambda bi, i: (bi, 0, 0)),
            pl.BlockSpec((1, n_mem, CX_W), lambda bi, i: (bi, 0, 0)),
            pl.BlockSpec((CX_W, d), lambda bi, i: (0, 0)),
        ],
        out_specs=pl.BlockSpec((1, bm, d), lambda bi, i: (bi, i, 0)),
        out_shape=jax.ShapeDtypeStruct((b, l, d), F32),
        scratch_shapes=[pltpu.VMEM((bm, d), BF16)],
        compiler_params=_cparams(("parallel", "arbitrary")),
        name="cross_attention",
    )(x, g_x.reshape(1, d), wq, q_gain.reshape(1, HEAD_DIM), kmem, vmem, wo)


def _pack_bf16_pairs(x):
    half = x.shape[1] // 2
    bits = lax.bitcast_convert_type(x.astype(BF16).astype(F32), jnp.uint32)
    return (bits[:, :half] & jnp.uint32(0xFFFF0000)) | (bits[:, half:] >> 16)


def _unpack_bf16_pairs(w):
    hi = lax.bitcast_convert_type(w & jnp.uint32(0xFFFF0000), F32)
    lo = lax.bitcast_convert_type(w << 16, F32)
    return hi, lo


def _router_kernel(x_ref, g_ref, rhi_ref, rlo_ref, idx_ref, w_ref, hp_ref, *, n_experts):
    hn = _rms_rows(x_ref[...], g_ref[...])
    hp_ref[...] = _pack_bf16_pairs(hn)
    hi = hn.astype(BF16)
    lo = (hn - hi.astype(F32)).astype(BF16)
    logits = (jnp.dot(hi, rhi_ref[...], preferred_element_type=F32)
              + jnp.dot(hi, rlo_ref[...], preferred_element_type=F32)
              + jnp.dot(lo, rhi_ref[...], preferred_element_type=F32))
    lane = lax.broadcasted_iota(jnp.int32, logits.shape, 1).astype(F32)
    low = jnp.float32(-3.0e38)
    l1 = jnp.where(lane < n_experts, logits, low)
    m1 = jnp.max(l1, axis=-1, keepdims=True)
    i1 = jnp.min(jnp.where(l1 == m1, lane, float(LANES)), axis=-1, keepdims=True)
    l2 = jnp.where(lane == i1, low, l1)
    m2 = jnp.max(l2, axis=-1, keepdims=True)
    i2 = jnp.min(jnp.where(l2 == m2, lane, float(LANES)), axis=-1, keepdims=True)
    e2 = jnp.exp(m2 - m1)
    w1 = 1.0 / (1.0 + e2)
    w2 = e2 / (1.0 + e2)
    idx_ref[...] = jnp.where(lane == 0.0, i1, jnp.where(lane == 1.0, i2, 0.0)).astype(jnp.int32)
    w_ref[...] = jnp.where(lane == 0.0, w1, jnp.where(lane == 1.0, w2, 0.0))


def route_tokens(x, gain, router):
    t, d = x.shape
    n_experts = router.shape[1]
    r = jnp.zeros((d, LANES), F32).at[:, :n_experts].set(router.astype(F32))
    r_hi = r.astype(BF16)
    r_lo = (r - r_hi.astype(F32)).astype(BF16)
    bm = _pick(t, (256, 128, 64, 32, 16, 8))
    return pl.pallas_call(
        functools.partial(_router_kernel, n_experts=n_experts),
        grid=(t // bm,),
        in_specs=[
            pl.BlockSpec((bm, d), lambda i: (i, 0)),
            pl.BlockSpec((1, d), lambda i: (0, 0)),
            pl.BlockSpec((d, LANES), lambda i: (0, 0)),
            pl.BlockSpec((d, LANES), lambda i: (0, 0)),
        ],
        out_specs=[pl.BlockSpec((bm, LANES), lambda i: (i, 0)), pl.BlockSpec((bm, LANES), lambda i: (i, 0)),
                   pl.BlockSpec((bm, d // 2), lambda i: (i, 0))],
        out_shape=[jax.ShapeDtypeStruct((t, LANES), jnp.int32), jax.ShapeDtypeStruct((t, LANES), F32),
                   jax.ShapeDtypeStruct((t, d // 2), jnp.uint32)],
        compiler_params=_cparams(("parallel",)),
        name="moe_router",
    )(x, gain.reshape(1, d), r_hi, r_lo)


def _swiglu_step(hn, wg_ref, wu_ref, wd_ref):
    a = jnp.dot(hn, wg_ref[0], preferred_element_type=F32)
    b = jnp.dot(hn, wu_ref[0], preferred_element_type=F32)
    hid = (a * jax.nn.sigmoid(a)) * b
    return jnp.dot(hid.astype(BF16), wd_ref[0], preferred_element_type=F32)


def _ffn_kernel(x_ref, g_ref, wg_ref, wu_ref, wd_ref, o_ref, hn_ref):
    @pl.when(pl.program_id(1) == 0)
    def _():
        _norm_block_to(x_ref, g_ref, hn_ref)
        o_ref[...] = x_ref[...]

    o_ref[...] += _swiglu_step(hn_ref[...], wg_ref, wu_ref, wd_ref)


def swiglu_ffn(x, gain, w_gu, w_down):
    t, d = x.shape
    ffn = w_down.shape[1]
    bm = _pick(t, (512, 256, 128, 64, 32, 16, 8))
    tf = _pick(ffn, (256, 128))
    nf = ffn // tf
    return pl.pallas_call(
        _ffn_kernel,
        grid=(t // bm, nf),
        in_specs=[
            pl.BlockSpec((bm, d), lambda i, f: (i, 0)),
            pl.BlockSpec((1, d), lambda i, f: (0, 0)),
            pl.BlockSpec((1, d, tf), lambda i, f: (0, 0, f)),
            pl.BlockSpec((1, d, tf), lambda i, f: (0, 0, nf + f)),
            pl.BlockSpec((1, tf, d), lambda i, f: (0, f, 0)),
        ],
        out_specs=pl.BlockSpec((bm, d), lambda i, f: (i, 0)),
        out_shape=jax.ShapeDtypeStruct((t, d), F32),
        scratch_shapes=[pltpu.VMEM((bm, d), BF16)],
        compiler_params=_cparams(("parallel", "arbitrary")),
        name="dense_ffn",
    )(x, gain.reshape(1, d), w_gu, w_gu, w_down)


MOE_TILE_ROWS = 512
MOE_MOVE_TOKENS = 256
MOE_ZERO_ROWS = 128
MOE_ISSUE_UNROLL = 8


def _moe_plan(top_idx, n_e, bm, n_tiles):
    e_flat = top_idx[:, :TOP_K].reshape(-1)
    onehot = (e_flat[:, None] == jnp.arange(n_e, dtype=jnp.int32)[None, :]).astype(jnp.int32)
    csum = jnp.cumsum(onehot, axis=0)
    cnt = csum[-1]
    rank = jnp.sum(onehot * (csum - 1), axis=1)
    padded = ((cnt + bm - 1) // bm) * bm
    gend = jnp.cumsum(padded)
    gstart = gend - padded
    dest = (jnp.sum(onehot * gstart[None, :], axis=1) + rank).astype(jnp.int32)
    n_valid = (gend[-1] // bm).astype(jnp.int32)
    tile_start = jnp.arange(n_tiles, dtype=jnp.int32) * bm
    tile_e = jnp.sum((tile_start[:, None] >= gend[None, :]).astype(jnp.int32), axis=1)
    tile_e = jnp.minimum(tile_e, n_e - 1)
    last_e = jnp.sum(jnp.where(jnp.arange(n_tiles) == n_valid - 1, tile_e, 0))
    tile_e = jnp.where(jnp.arange(n_tiles) < n_valid, tile_e, last_e).astype(jnp.int32)
    return dest, tile_e, n_valid.reshape(1), gend.astype(jnp.int32), cnt.astype(jnp.int32)


def _dispatch_kernel(gend_ref, cnt_ref, nv_ref, dest_ref, x_ref, xs_hbm, zbuf, sem, zsem, *,
                     bt, bm, n_e, n_tiles):
    i = pl.program_id(0)

    @pl.when(i == 0)
    def _():
        zbuf[...] = jnp.zeros_like(zbuf)
        zr = zbuf.shape[0]

        def clear_tile(base):
            copies = [pltpu.make_async_copy(
                zbuf, xs_hbm.at[pl.ds(pl.multiple_of(base + z * zr, zr), zr)], zsem)
                for z in range(bm // zr)]
            for cp in copies:
                cp.start()
            for cp in copies:
                cp.wait()

        for e in range(n_e):
            @pl.when(cnt_ref[e] > 0)
            def _():
                clear_tile(gend_ref[e] - bm)

        for back in range(1, min(n_e, n_tiles) + 1):
            @pl.when(n_tiles - back >= nv_ref[0])
            def _():
                clear_tile((n_tiles - back) * bm)

    def issue(r, carry):
        for k in range(TOP_K):
            pltpu.make_async_copy(x_ref.at[pl.ds(r, 1)],
                                  xs_hbm.at[pl.ds(dest_ref[0, 0, TOP_K * r + k], 1)], sem).start()
        return carry

    lax.fori_loop(0, bt, issue, 0, unroll=MOE_ISSUE_UNROLL)
    for k in range(TOP_K):
        pltpu.make_async_copy(x_ref, xs_hbm.at[pl.ds(0, bt)], sem).wait()


def moe_dispatch(x, dest, gend, cnt, n_valid, n_tiles, bm):
    t, d = x.shape
    n_rows = n_tiles * bm
    bt = _pick(t, (MOE_MOVE_TOKENS, 64, 32, 16, 8))
    n_steps = t // bt
    zr = min(MOE_ZERO_ROWS, bm)
    return pl.pallas_call(
        functools.partial(_dispatch_kernel, bt=bt, bm=bm, n_e=cnt.shape[0], n_tiles=n_tiles),
        grid_spec=pltpu.PrefetchScalarGridSpec(
            num_scalar_prefetch=3,
            grid=(n_steps,),
            in_specs=[
                pl.BlockSpec((1, 1, TOP_K * bt), lambda i, ge, cn, nv: (i, 0, 0), memory_space=pltpu.SMEM),
                pl.BlockSpec((bt, d), lambda i, ge, cn, nv: (i, 0)),
            ],
            out_specs=pl.BlockSpec(memory_space=pl.ANY),
            scratch_shapes=[pltpu.VMEM((zr, d), x.dtype), pltpu.SemaphoreType.DMA(()),
                            pltpu.SemaphoreType.DMA(())],
        ),
        out_shape=jax.ShapeDtypeStruct((n_rows, d), x.dtype),
        compiler_params=_cparams(("arbitrary",)),
        name="moe_dispatch",
    )(gend, cnt, n_valid, dest.reshape(n_steps, 1, TOP_K * bt), x)


def _moe_ffn_kernel(te_ref, nv_ref, x_ref, wg_ref, wu_ref, wd_ref, o_ref, hn_ref, acc_ref, *, nf, slab):
    j = pl.program_id(0)
    f = pl.program_id(1)
    valid = j < nv_ref[0]
    rows, half = x_ref.shape

    @pl.when(valid & (f == 0))
    def _():
        def unpack(r, carry):
            r0 = pl.multiple_of(r * slab, slab)
            hi, lo = _unpack_bf16_pairs(x_ref[pl.ds(r0, slab), :])
            hn_ref[pl.ds(r0, slab), pl.ds(0, half)] = hi.astype(hn_ref.dtype)
            hn_ref[pl.ds(r0, slab), pl.ds(half, half)] = lo.astype(hn_ref.dtype)
            return carry

        lax.fori_loop(0, rows // slab, unpack, 0)
        acc_ref[...] = _swiglu_step(hn_ref[...], wg_ref, wu_ref, wd_ref)

    @pl.when(valid & (f > 0))
    def _():
        acc_ref[...] += _swiglu_step(hn_ref[...], wg_ref, wu_ref, wd_ref)

    @pl.when(valid & (f == nf - 1))
    def _():
        def pack(r, carry):
            r0 = pl.multiple_of(r * slab, slab)
            o_ref[pl.ds(r0, slab), :] = _pack_bf16_pairs(acc_ref[pl.ds(r0, slab), :])
            return carry

        lax.fori_loop(0, rows // slab, pack, 0)

    @pl.when(jnp.logical_not(valid) & (f == nf - 1))
    def _():
        o_ref[...] = jnp.zeros_like(o_ref)


def moe_expert_ffn(xs, w_gu, w_down, tile_e, n_valid, bm):
    n_rows, half = xs.shape
    d = 2 * half
    n_tiles = n_rows // bm
    slab = min(64, bm)
    ffn = w_down.shape[1]
    tf = _pick(ffn, (256, 128))
    nf = ffn // tf

    def fidx(j, f, nv):
        return jnp.where(j < nv[0], f, nf - 1)

    return pl.pallas_call(
        functools.partial(_moe_ffn_kernel, nf=nf, slab=slab),
        grid_spec=pltpu.PrefetchScalarGridSpec(
            num_scalar_prefetch=2,
            grid=(n_tiles, nf),
            in_specs=[
                pl.BlockSpec((bm, half), lambda j, f, te, nv: (jnp.minimum(j, nv[0] - 1), 0)),
                pl.BlockSpec((1, d, tf), lambda j, f, te, nv: (te[j], 0, fidx(j, f, nv))),
                pl.BlockSpec((1, d, tf), lambda j, f, te, nv: (te[j], 0, nf + fidx(j, f, nv))),
                pl.BlockSpec((1, tf, d), lambda j, f, te, nv: (te[j], fidx(j, f, nv), 0)),
            ],
            out_specs=pl.BlockSpec((bm, half), lambda j, f, te, nv: (j, 0)),
            scratch_shapes=[pltpu.VMEM((bm, d), BF16), pltpu.VMEM((bm, d), F32)],
        ),
        out_shape=jax.ShapeDtypeStruct((n_rows, half), jnp.uint32),
        compiler_params=_cparams(("arbitrary", "arbitrary")),
        name="moe_ffn",
    )(tile_e, n_valid, xs, w_gu, w_gu, w_down)


def _combine_kernel(pos_ref, posn_ref, x_ref, w_ref, y_hbm, o_ref, ybuf, sems, *, bt, n_steps):
    i = pl.program_id(0)
    slot = i % 2

    def issue(p_ref, s):
        def body(r, carry):
            for k in range(TOP_K):
                pltpu.make_async_copy(y_hbm.at[pl.ds(p_ref[0, 0, TOP_K * r + k], 1)],
                                      ybuf.at[s, k, pl.ds(r, 1)], sems.at[s]).start()
            return carry

        lax.fori_loop(0, bt, body, 0, unroll=MOE_ISSUE_UNROLL)

    @pl.when(i == 0)
    def _():
        issue(pos_ref, 0)

    @pl.when(i + 1 < n_steps)
    def _():
        issue(posn_ref, 1 - slot)

    for k in range(TOP_K):
        pltpu.make_async_copy(y_hbm.at[pl.ds(0, bt)], ybuf.at[slot, k], sems.at[slot]).wait()
    w = w_ref[...]
    half = ybuf.shape[-1]
    a_hi, a_lo = _unpack_bf16_pairs(ybuf[slot, 0])
    b_hi, b_lo = _unpack_bf16_pairs(ybuf[slot, 1])
    o_ref[:, :half] = x_ref[:, :half] + w[:, 0:1] * a_hi + w[:, 1:2] * b_hi
    o_ref[:, half:] = x_ref[:, half:] + w[:, 0:1] * a_lo + w[:, 1:2] * b_lo


def moe_combine(x, top_w, dest, ys):
    t, d = x.shape
    bt = _pick(t, (MOE_MOVE_TOKENS, 64, 32, 16, 8))
    n_steps = t // bt
    pos = dest.reshape(n_steps, 1, TOP_K * bt)
    return pl.pallas_call(
        functools.partial(_combine_kernel, bt=bt, n_steps=n_steps),
        grid=(n_steps,),
        in_specs=[
            pl.BlockSpec((1, 1, TOP_K * bt), lambda i: (i, 0, 0), memory_space=pltpu.SMEM),
            pl.BlockSpec((1, 1, TOP_K * bt), lambda i: (jnp.minimum(i + 1, n_steps - 1), 0, 0),
                         memory_space=pltpu.SMEM),
            pl.BlockSpec((bt, d), lambda i: (i, 0)),
            pl.BlockSpec((bt, LANES), lambda i: (i, 0)),
            pl.BlockSpec(memory_space=pl.ANY),
        ],
        out_specs=pl.BlockSpec((bt, d), lambda i: (i, 0)),
        out_shape=jax.ShapeDtypeStruct((t, d), F32),
        scratch_shapes=[pltpu.VMEM((2, TOP_K, bt, d // 2), jnp.uint32), pltpu.SemaphoreType.DMA((2,))],
        compiler_params=_cparams(("arbitrary",)),
        name="moe_combine",
    )(pos, pos, x, top_w, ys)


def moe_ffn(x, gain, router, w_gu, w_down):
    t, d = x.shape
    n_e = w_down.shape[0]
    bm = min(MOE_TILE_ROWS, max(8, t // 8))
    n_tiles = (TOP_K * t + n_e * (bm - 1) + bm - 1) // bm
    top_idx, top_w, hn_packed = route_tokens(x, gain, router)
    dest, tile_e, n_valid, gend, cnt = _moe_plan(top_idx, n_e, bm, n_tiles)
    xs = moe_dispatch(hn_packed, dest, gend, cnt, n_valid, n_tiles, bm)
    ys = moe_expert_ffn(xs, w_gu, w_down, tile_e, n_valid, bm)
    return moe_combine(x, top_w, dest, ys)


def _prep_layer(i, p):
    w_in = p["w_in"][i]
    d = w_in.shape[0]
    w_main = jnp.concatenate([w_in[:, :OFF_GATE_END], w_in[:, OFF_DECAY_END:]], axis=1).astype(BF16)
    w_decay = jnp.zeros((d, LANES), BF16).at[:, :2 * GLA_RANK].set(
        w_in[:, OFF_GATE_END:OFF_DECAY_END].astype(BF16))

    def up_mat(up, row0):
        u = up.reshape(GLA_RANK, GLA_HEADS, GLA_DK).transpose(1, 0, 2).astype(BF16)
        return jnp.zeros((GLA_HEADS, LANES, GLA_DK), BF16).at[:, row0:row0 + GLA_RANK, :].set(u)

    return dict(
        norm_mix=p["norm_mix"][i], w_main=w_main, w_decay=w_decay,
        u_f=up_mat(p["gla_up_f"][i], 0), u_b=up_mat(p["gla_up_b"][i], GLA_RANK),
        bias_f=p["gla_bias_f"][i].reshape(GLA_HEADS, 1, GLA_DK).astype(F32),
        bias_b=p["gla_bias_b"][i].reshape(GLA_HEADS, 1, GLA_DK).astype(F32),
        gla_out_norm=p["gla_out_norm"][i], swa_q_norm=p["swa_q_norm"][i], swa_k_norm=p["swa_k_norm"][i],
        swa_sink=p["swa_sink"][i], w_out=p["w_out"][i].astype(BF16),
        norm_cross=p["norm_cross"][i], norm_mem=p["norm_mem"][i],
        cx_wq=p["cx_wq"][i].astype(BF16), cx_wkv=p["cx_wkv"][i].astype(BF16),
        cx_q_norm=p["cx_q_norm"][i], cx_k_norm=p["cx_k_norm"][i], cx_wo=p["cx_wo"][i].astype(BF16),
        norm_ffn=p["norm_ffn"][i],
    )


def _trunk(x, mem, layers, bias_tab):
    b, l, d = x.shape
    t = b * l
    for i, lw in enumerate(layers):
        proj, decay_in = in_proj(x.reshape(t, d), lw["norm_mix"], lw["w_main"], lw["w_decay"])
        proj = proj.reshape(b, l, MAIN_W)
        decay_in = decay_in.reshape(b, l, LANES)
        o_f = gla_direction(proj, decay_in, lw["u_f"], lw["bias_f"], reverse=False)
        o_gla = gla_direction(proj, decay_in, lw["u_b"], lw["bias_b"], reverse=True,
                              o_fwd=o_f, out_gain=lw["gla_out_norm"])
        o_swa = swa_mixer(proj, bias_tab, lw["swa_sink"], lw["swa_q_norm"], lw["swa_k_norm"])
        x1 = out_proj(o_gla.reshape(t, GLA_V_W), o_swa.reshape(t, SWA_Q_W), lw["w_out"], x.reshape(t, d))
        kmem, vmem = cross_kv(mem, lw["norm_mem"], lw["cx_wkv"], lw["cx_k_norm"])
        x2 = cross_attention(x1.reshape(b, l, d), kmem, vmem, lw["norm_cross"], lw["cx_wq"],
                             lw["cx_q_norm"], lw["cx_wo"]).reshape(t, d)
        if i % 2 == 0:
            x3 = swiglu_ffn(x2, lw["norm_ffn"], lw["ffn_w_gu"], lw["ffn_w_down"])
        else:
            x3 = moe_ffn(x2, lw["norm_ffn"], lw["moe_router"], lw["moe_w_gu"], lw["moe_w_down"])
        x = x3.reshape(b, l, d)
    return x


def kernel(x_prompt, x_sample, mem_prompt, mem_sample, rel_bias, norm_mix, w_in, gla_up_f, gla_bias_f, gla_up_b, gla_bias_b, gla_out_norm, swa_q_norm, swa_k_norm, swa_sink, w_out, norm_cross, norm_mem, cx_wq, cx_wkv, cx_q_norm, cx_k_norm, cx_wo, norm_ffn, ffn_w_gu, ffn_w_down, moe_router, moe_w_gu, moe_w_down):
    p = dict(norm_mix=norm_mix, w_in=w_in, gla_up_f=gla_up_f, gla_bias_f=gla_bias_f, gla_up_b=gla_up_b,
             gla_bias_b=gla_bias_b, gla_out_norm=gla_out_norm, swa_q_norm=swa_q_norm, swa_k_norm=swa_k_norm,
             swa_sink=swa_sink, w_out=w_out, norm_cross=norm_cross, norm_mem=norm_mem, cx_wq=cx_wq,
             cx_wkv=cx_wkv, cx_q_norm=cx_q_norm, cx_k_norm=cx_k_norm, cx_wo=cx_wo, norm_ffn=norm_ffn)
    depth = w_in.shape[0]
    layers = []
    for i in range(depth):
        lw = _prep_layer(i, p)
        if i % 2 == 0:
            lw["ffn_w_gu"] = ffn_w_gu[i // 2][None].astype(BF16)
            lw["ffn_w_down"] = ffn_w_down[i // 2][None].astype(BF16)
        else:
            lw["moe_router"] = moe_router[i // 2]
            lw["moe_w_gu"] = moe_w_gu[i // 2].astype(BF16)
            lw["moe_w_down"] = moe_w_down[i // 2].astype(BF16)
        layers.append(lw)
    bias_tab = swa_bias_table(rel_bias)
    y_prompt = _trunk(x_prompt, mem_prompt, layers, bias_tab)
    y_sample = _trunk(x_sample, mem_sample, layers, bias_tab)
    return (y_prompt, y_sample)
```

```python
import functools

import numpy as np
import jax
import jax.numpy as jnp
from jax import lax
from jax.experimental import pallas as pl
from jax.experimental.pallas import tpu as pltpu

F32 = jnp.float32
BF16 = jnp.bfloat16

HEAD_DIM = 128
GLA_HEADS = 8
GLA_DK = 128
GLA_DV = 256
GLA_RANK = 16
GLA_TAU = 16.0
GLA_CHUNK = 64
GLA_HEADS_PER_STEP = 4
SWA_HQ = 16
SWA_HKV = 4
SWA_GROUP = SWA_HQ // SWA_HKV
WINDOW = 128
N_BUCKETS = 32
MAX_DISTANCE = 128
CX_HEADS = 4
TOP_K = 2
EPS = 1e-6
NEG_INF = -1e30

GLA_QK_W = GLA_HEADS * GLA_DK
GLA_V_W = GLA_HEADS * GLA_DV
SWA_Q_W = SWA_HQ * HEAD_DIM
SWA_KV_W = SWA_HKV * HEAD_DIM
CX_W = CX_HEADS * HEAD_DIM
OFF_GATE_END = 2 * GLA_QK_W + 2 * GLA_V_W
OFF_DECAY_END = OFF_GATE_END + 2 * GLA_RANK
MAIN_W = OFF_GATE_END + SWA_Q_W + 2 * SWA_KV_W

LANES = 128
VMEM_LIMIT_BYTES = 56 * 1024 * 1024


def _cparams(sem, vmem_limit_bytes=VMEM_LIMIT_BYTES):
    return pltpu.CompilerParams(dimension_semantics=sem, vmem_limit_bytes=vmem_limit_bytes)


def _rms_rows(x, gain):
    ms = jnp.mean(x * x, axis=-1, keepdims=True)
    return x * lax.rsqrt(ms + EPS) * gain


def _norm_block_to(x_ref, g_ref, hn_ref, slab=16):
    rows = x_ref.shape[0]
    slab = min(slab, rows)

    def body(r, carry):
        r0 = pl.multiple_of(r * slab, slab)
        hn_ref[pl.ds(r0, slab), :] = _rms_rows(x_ref[pl.ds(r0, slab), :], g_ref[...]).astype(hn_ref.dtype)
        return carry

    lax.fori_loop(0, rows // slab, body, 0, unroll=8)


def _pick(n, cands):
    for c in cands:
        if n % c == 0:
            return c
    return n


def _in_proj_kernel(x_ref, g_ref, w_ref, wa_ref, o_ref, oa_ref, hn_ref):
    @pl.when(pl.program_id(1) == 0)
    def _():
        _norm_block_to(x_ref, g_ref, hn_ref)
        oa_ref[...] = jnp.dot(hn_ref[...], wa_ref[...], preferred_element_type=F32)

    o_ref[...] = jnp.dot(hn_ref[...], w_ref[...], preferred_element_type=F32).astype(o_ref.dtype)


def in_proj(x, gain, w_main, w_decay):
    t, d = x.shape
    n = w_main.shape[1]
    bm = _pick(t, (512, 256, 128, 64, 32, 16, 8))
    bn = _pick(n, (1536, 1024, 768, 512, 256, 128))
    return pl.pallas_call(
        _in_proj_kernel,
        grid=(t // bm, n // bn),
        in_specs=[
            pl.BlockSpec((bm, d), lambda i, j: (i, 0)),
            pl.BlockSpec((1, d), lambda i, j: (0, 0)),
            pl.BlockSpec((d, bn), lambda i, j: (0, j)),
            pl.BlockSpec((d, LANES), lambda i, j: (0, 0)),
        ],
        out_specs=[
            pl.BlockSpec((bm, bn), lambda i, j: (i, j)),
            pl.BlockSpec((bm, LANES), lambda i, j: (i, 0)),
        ],
        out_shape=[jax.ShapeDtypeStruct((t, n), BF16), jax.ShapeDtypeStruct((t, LANES), F32)],
        scratch_shapes=[pltpu.VMEM((bm, d), BF16)],
        compiler_params=_cparams(("parallel", "arbitrary")),
        name="in_proj",
    )(x, gain.reshape(1, d), w_main, w_decay)


def _log_sigmoid(z):
    return jnp.minimum(z, 0.0) - jnp.log1p(jnp.exp(-jnp.abs(z)))


def _gla_kernel(*refs, reverse, finalize, rows):
    if finalize:
        q_ref, k_ref, v_ref, a_ref, u_ref, b_ref, tri_ref, msk_ref, of_ref, g_ref, gain_ref, o_ref, st_ref = refs
    else:
        q_ref, k_ref, v_ref, a_ref, u_ref, b_ref, tri_ref, msk_ref, o_ref, st_ref = refs
    c = GLA_CHUNK
    nch = rows // c
    nh = st_ref.shape[0]
    hk = [slice(i * GLA_DK, (i + 1) * GLA_DK) for i in range(nh)]
    hv = [slice(i * GLA_DV, (i + 1) * GLA_DV) for i in range(nh)]

    @pl.when(pl.program_id(2) == 0)
    def _():
        st_ref[...] = jnp.zeros_like(st_ref)

    scale = GLA_DK ** -0.5
    nt = (((1,), (1,)), ((), ()))
    tn = (((0,), (0,)), ((), ()))

    a = a_ref[0].astype(BF16)
    zs = [jnp.dot(a, u_ref[i], preferred_element_type=F32) + b_ref[i] for i in range(nh)]
    las = [_log_sigmoid(z) * (1.0 / GLA_TAU) for z in zs]

    hls = []
    for la in las:
        la_hi = la.astype(BF16)
        la_lo = (la - la_hi.astype(F32)).astype(BF16)
        hls.append(jnp.concatenate([la_hi, la_lo], axis=1))
    cum2s = [jnp.dot(tri_ref[...], hl, preferred_element_type=F32) for hl in hls]

    q_ins, k_ins, k_ends, lasts_all = [], [], [], []
    for i, cum2 in enumerate(cum2s):
        cum = cum2[:, :GLA_DK] + cum2[:, GLA_DK:]
        lasts = [cum[ch * c:ch * c + 1] if reverse else cum[(ch + 1) * c - 1:(ch + 1) * c]
                 for ch in range(nch)]
        last_rows = jnp.concatenate([jnp.broadcast_to(l, (c, GLA_DK)) for l in lasts], axis=0)
        q = q_ref[0, :, hk[i]].astype(F32)
        k = k_ref[0, :, hk[i]].astype(F32)
        q_ins.append(((q * scale) * jnp.exp(cum)).astype(BF16))
        k_ins.append((k * jnp.exp(-cum)).astype(BF16))
        k_ends.append((k * jnp.exp(last_rows - cum)).astype(BF16))
        lasts_all.append(lasts)

    scs = [lax.dot_general(q_in, k_in, nt, preferred_element_type=F32) for q_in, k_in in zip(q_ins, k_ins)]
    scs = [jnp.where(msk_ref[...] > 0.5, sc, 0.0).astype(BF16) for sc in scs]
    o_intras = [jnp.dot(sc, v_ref[0, :, hv[i]], preferred_element_type=F32) for i, sc in enumerate(scs)]
    kvs_all = [[lax.dot_general(v_ref[0, ch * c:(ch + 1) * c, hv[i]], k_ends[i][ch * c:(ch + 1) * c], tn,
                                preferred_element_type=F32) for ch in range(nch)]
               for i in range(nh)]

    order = range(nch - 1, -1, -1) if reverse else range(nch)
    for i in range(nh):
        st = st_ref[i]
        for ch in order:
            sl = slice(ch * c, (ch + 1) * c)
            o = o_intras[i][sl] + lax.dot_general(q_ins[i][sl], st.astype(BF16), nt,
                                                  preferred_element_type=F32)
            st = st * jnp.exp(lasts_all[i][ch]) + kvs_all[i][ch]
            if finalize:
                o = o + of_ref[0, sl, hv[i]]
                o = _rms_rows(o, gain_ref[...])
                g = g_ref[0, sl, hv[i]].astype(F32)
                o_ref[0, sl, hv[i]] = (o * (g * jax.nn.sigmoid(g))).astype(o_ref.dtype)
            else:
                o_ref[0, sl, hv[i]] = o
        st_ref[i] = st


def gla_direction(proj, decay_in, u, bias, *, reverse, o_fwd=None, out_gain=None):
    b, l, _ = proj.shape
    rows = _pick(l, (512, 256, 128, 64))
    nb = l // rows
    finalize = o_fwd is not None
    blk = (lambda s: nb - 1 - s) if reverse else (lambda s: s)
    nh = GLA_HEADS_PER_STEP
    wk, wv = nh * GLA_DK, nh * GLA_DV
    kq = GLA_QK_W // wk
    kv = 2 * GLA_QK_W // wv
    kg = kv + GLA_HEADS // nh
    in_specs = [
        pl.BlockSpec((1, rows, wk), lambda bi, h, s: (bi, blk(s), h)),
        pl.BlockSpec((1, rows, wk), lambda bi, h, s: (bi, blk(s), kq + h)),
        pl.BlockSpec((1, rows, wv), lambda bi, h, s: (bi, blk(s), kv + h)),
        pl.BlockSpec((1, rows, LANES), lambda bi, h, s: (bi, blk(s), 0)),
        pl.BlockSpec((nh, LANES, GLA_DK), lambda bi, h, s: (h, 0, 0)),
        pl.BlockSpec((nh, 1, GLA_DK), lambda bi, h, s: (h, 0, 0)),
        pl.BlockSpec((rows, rows), lambda bi, h, s: (0, 0)),
        pl.BlockSpec((rows, rows), lambda bi, h, s: (0, 0)),
    ]
    ri = np.arange(rows)[:, None]
    ci = np.arange(rows)[None, :]
    same = (ri // GLA_CHUNK) == (ci // GLA_CHUNK)
    tri = same & ((ci >= ri) if reverse else (ci <= ri))
    msk = same & ((ci > ri) if reverse else (ci <= ri))
    args = [proj, proj, proj, decay_in, u, bias, jnp.asarray(tri, BF16), jnp.asarray(msk, F32)]
    if finalize:
        in_specs += [
            pl.BlockSpec((1, rows, wv), lambda bi, h, s: (bi, blk(s), h)),
            pl.BlockSpec((1, rows, wv), lambda bi, h, s: (bi, blk(s), kg + h)),
            pl.BlockSpec((1, GLA_DV), lambda bi, h, s: (0, 0)),
        ]
        args += [o_fwd, proj, out_gain.reshape(1, GLA_DV)]
    out_dtype = BF16 if finalize else F32
    return pl.pallas_call(
        functools.partial(_gla_kernel, reverse=reverse, finalize=finalize, rows=rows),
        grid=(b, GLA_HEADS // nh, nb),
        in_specs=in_specs,
        out_specs=pl.BlockSpec((1, rows, wv), lambda bi, h, s: (bi, blk(s), h)),
        out_shape=jax.ShapeDtypeStruct((b, l, GLA_V_W), out_dtype),
        scratch_shapes=[pltpu.VMEM((nh, GLA_DV, GLA_DK), F32)],
        compiler_params=_cparams(("parallel", "parallel", "arbitrary")),
        name="gla_bwd" if reverse else "gla_fwd",
    )(*args)


def _t5_buckets(rel):
    half = N_BUCKETS // 2
    ret = (rel > 0).astype(np.int32) * half
    n = np.abs(rel)
    max_exact = half // 2
    large = max_exact + (np.log(np.maximum(n, 1) / max_exact) / np.log(MAX_DISTANCE / max_exact)
                         * (half - max_exact)).astype(np.int32)
    large = np.minimum(large, half - 1)
    return ret + np.where(n < max_exact, n, large)


def _bias_table_kernel(rb_ref, bkt_ref, o_ref):
    h = pl.program_id(0)
    bkt = bkt_ref[...]
    for g in range(SWA_GROUP):
        acc = jnp.zeros(bkt.shape, F32)
        for b in range(N_BUCKETS):
            acc = jnp.where(bkt == b, rb_ref[b, h * SWA_GROUP + g], acc)
        o_ref[0, g * WINDOW:(g + 1) * WINDOW, :] = acc


def swa_bias_table(rel_bias):
    rel = np.arange(3 * WINDOW)[None, :] - WINDOW - np.arange(WINDOW)[:, None]
    bkt = jnp.asarray(_t5_buckets(rel), jnp.int32)
    return pl.pallas_call(
        _bias_table_kernel,
        grid=(SWA_HKV,),
        in_specs=[
            pl.BlockSpec(memory_space=pltpu.SMEM),
            pl.BlockSpec((WINDOW, 3 * WINDOW), lambda h: (0, 0)),
        ],
        out_specs=pl.BlockSpec((1, SWA_GROUP * WINDOW, 3 * WINDOW), lambda h: (h, 0, 0)),
        out_shape=jax.ShapeDtypeStruct((SWA_HKV, SWA_GROUP * WINDOW, 3 * WINDOW), F32),
        compiler_params=_cparams(("parallel",)),
        name="swa_bias_table",
    )(rel_bias.astype(F32), bkt)


def _swa_kernel(sink_ref, q_ref, kp_ref, kc_ref, kn_ref, vp_ref, vc_ref, vn_ref, bias_ref,
                qg_ref, kg_ref, o_ref, *, nb):
    i = pl.program_id(1)
    w = WINDOW
    gw = SWA_GROUP * HEAD_DIM
    row = lax.broadcasted_iota(jnp.int32, (w, 3 * w), 0)
    col = lax.broadcasted_iota(jnp.int32, (w, 3 * w), 1)
    rel = col - w - row
    valid = (jnp.abs(rel) <= w) & ((col >= w) | (i > 0)) & ((col < 2 * w) | (i < nb - 1))
    heads = [(h, g) for h in range(SWA_HKV) for g in range(SWA_GROUP)]
    kns, vs = [], []
    for h in range(SWA_HKV):
        hs = slice(h * HEAD_DIM, (h + 1) * HEAD_DIM)
        k = jnp.concatenate([kp_ref[0, :, hs], kc_ref[0, :, hs], kn_ref[0, :, hs]], axis=0).astype(F32)
        kns.append(_rms_rows(k, kg_ref[...]).astype(BF16))
        vs.append(jnp.concatenate([vp_ref[0, :, hs], vc_ref[0, :, hs], vn_ref[0, :, hs]], axis=0))
    scores = []
    for h, g in heads:
        qs = slice(h * gw + g * HEAD_DIM, h * gw + (g + 1) * HEAD_DIM)
        qn = _rms_rows(q_ref[0, :, qs].astype(F32), qg_ref[...]).astype(BF16)
        scores.append(lax.dot_general(qn, kns[h], (((1,), (1,)), ((), ())), preferred_element_type=F32))
    probs = []
    for (h, g), s in zip(heads, scores):
        logits = s * (HEAD_DIM ** -0.5) + bias_ref[h, g * w:(g + 1) * w, :]
        logits = jnp.where(valid, logits, NEG_INF)
        sink = sink_ref[h * SWA_GROUP + g]
        m = jnp.maximum(jnp.max(logits, axis=-1, keepdims=True), sink)
        p = jnp.exp(logits - m)
        denom = jnp.sum(p, axis=-1, keepdims=True) + jnp.exp(sink - m)
        probs.append((p.astype(BF16), denom))
    for (h, g), (p, denom) in zip(heads, probs):
        qs = slice(h * gw + g * HEAD_DIM, h * gw + (g + 1) * HEAD_DIM)
        o = jnp.dot(p, vs[h], preferred_element_type=F32) / denom
        o_ref[0, :, qs] = o.astype(o_ref.dtype)


def swa_mixer(proj, bias_tab, sink, q_gain, k_gain):
    b, l, _ = proj.shape
    nb = l // WINDOW
    q0 = OFF_GATE_END // SWA_Q_W
    k0 = (OFF_GATE_END + SWA_Q_W) // SWA_KV_W
    v0 = k0 + 1
    assert OFF_GATE_END % SWA_Q_W == 0 and (OFF_GATE_END + SWA_Q_W) % SWA_KV_W == 0

    def kv_spec(base, shift):
        return pl.BlockSpec(
            (1, WINDOW, SWA_KV_W),
            lambda bi, i: (bi, jnp.clip(i + shift, 0, nb - 1), base))

    return pl.pallas_call(
        functools.partial(_swa_kernel, nb=nb),
        grid=(b, nb),
        in_specs=[
            pl.BlockSpec(memory_space=pltpu.SMEM),
            pl.BlockSpec((1, WINDOW, SWA_Q_W), lambda bi, i: (bi, i, q0)),
            kv_spec(k0, -1), kv_spec(k0, 0), kv_spec(k0, 1),
            kv_spec(v0, -1), kv_spec(v0, 0), kv_spec(v0, 1),
            pl.BlockSpec((SWA_HKV, SWA_GROUP * WINDOW, 3 * WINDOW), lambda bi, i: (0, 0, 0)),
            pl.BlockSpec((1, HEAD_DIM), lambda bi, i: (0, 0)),
            pl.BlockSpec((1, HEAD_DIM), lambda bi, i: (0, 0)),
        ],
        out_specs=pl.BlockSpec((1, WINDOW, SWA_Q_W), lambda bi, i: (bi, i, 0)),
        out_shape=jax.ShapeDtypeStruct((b, l, SWA_Q_W), BF16),
        compiler_params=_cparams(("parallel", "arbitrary")),
        name="swa",
    )(sink.astype(F32), proj, proj, proj, proj, proj, proj, proj, bias_tab,
      q_gain.reshape(1, HEAD_DIM), k_gain.reshape(1, HEAD_DIM))


def _out_proj_kernel(a_ref, b_ref, wa_ref, wb_ref, x_ref, o_ref):
    o_ref[...] = (x_ref[...]
                  + jnp.dot(a_ref[...], wa_ref[...], preferred_element_type=F32)
                  + jnp.dot(b_ref[...], wb_ref[...], preferred_element_type=F32))


def out_proj(o_gla, o_swa, w_out, x):
    t, d = x.shape
    ka, kb = o_gla.shape[1], o_swa.shape[1]
    bm = _pick(t, (1024, 512, 256, 128, 64, 32, 16, 8))
    bn = _pick(d, (1024, 512, 256, 128))
    assert ka == kb and w_out.shape[0] == ka + kb
    return pl.pallas_call(
        _out_proj_kernel,
        grid=(t // bm, d // bn),
        in_specs=[
            pl.BlockSpec((bm, ka), lambda i, j: (i, 0)),
            pl.BlockSpec((bm, kb), lambda i, j: (i, 0)),
            pl.BlockSpec((ka, bn), lambda i, j: (0, j)),
            pl.BlockSpec((kb, bn), lambda i, j: (1, j)),
            pl.BlockSpec((bm, bn), lambda i, j: (i, j)),
        ],
        out_specs=pl.BlockSpec((bm, bn), lambda i, j: (i, j)),
        out_shape=jax.ShapeDtypeStruct((t, d), F32),
        compiler_params=_cparams(("parallel", "arbitrary")),
        name="out_proj",
    )(o_gla, o_swa, w_out, w_out, x)


def _cross_kv_kernel(m_ref, g_ref, w_ref, kg_ref, k_ref, v_ref, hn_ref):
    _norm_block_to(m_ref.at[0], g_ref, hn_ref)
    kv = jnp.dot(hn_ref[...], w_ref[...], preferred_element_type=F32)
    for h in range(CX_HEADS):
        kh = kv[:, h * HEAD_DIM:(h + 1) * HEAD_DIM]
        k_ref[0, :, h * HEAD_DIM:(h + 1) * HEAD_DIM] = _rms_rows(kh, kg_ref[...]).astype(k_ref.dtype)
    v_ref[0] = kv[:, CX_W:].astype(v_ref.dtype)


def cross_kv(mem, g_m, wkv, k_gain):
    b, n_mem, d = mem.shape
    return pl.pallas_call(
        _cross_kv_kernel,
        grid=(b,),
        in_specs=[
            pl.BlockSpec((1, n_mem, d), lambda i: (i, 0, 0)),
            pl.BlockSpec((1, d), lambda i: (0, 0)),
            pl.BlockSpec((d, 2 * CX_W), lambda i: (0, 0)),
            pl.BlockSpec((1, HEAD_DIM), lambda i: (0, 0)),
        ],
        out_specs=[
            pl.BlockSpec((1, n_mem, CX_W), lambda i: (i, 0, 0)),
            pl.BlockSpec((1, n_mem, CX_W), lambda i: (i, 0, 0)),
        ],
        out_shape=[jax.ShapeDtypeStruct((b, n_mem, CX_W), BF16)] * 2,
        scratch_shapes=[pltpu.VMEM((n_mem, d), BF16)],
        compiler_params=_cparams(("parallel",)),
        name="cross_kv",
    )(mem, g_m.reshape(1, d), wkv, k_gain.reshape(1, HEAD_DIM))


def _cross_kernel(x_ref, g_ref, wq_ref, qg_ref, k_ref, v_ref, wo_ref, o_ref, hn_ref):
    _norm_block_to(x_ref.at[0], g_ref, hn_ref)
    q = jnp.dot(hn_ref[...], wq_ref[...], preferred_element_type=F32)
    hsl = [slice(h * HEAD_DIM, (h + 1) * HEAD_DIM) for h in range(CX_HEADS)]
    qhs = [_rms_rows(q[:, hs], qg_ref[...]).astype(BF16) for hs in hsl]
    ss = [lax.dot_general(qh, k_ref[0, :, hs], (((1,), (1,)), ((), ())), preferred_element_type=F32)
          for qh, hs in zip(qhs, hsl)]
    ps = []
    for s in ss:
        s = s * (HEAD_DIM ** -0.5)
        m = jnp.max(s, axis=-1, keepdims=True)
        p = jnp.exp(s - m)
        ps.append((p.astype(BF16), jnp.sum(p, axis=-1, keepdims=True)))
    outs = [jnp.dot(p, v_ref[0, :, hs], preferred_element_type=F32) / denom
            for (p, denom), hs in zip(ps, hsl)]
    o = jnp.concatenate(outs, axis=1).astype(BF16)
    o_ref[0] = x_ref[0] + jnp.dot(o, wo_ref[...], preferred_element_type=F32)


def cross_attention(x, kmem, vmem, g_x, wq, q_gain, wo):
    b, l, d = x.shape
    n_mem = kmem.shape[1]
    bm = _pick(l, (512, 256, 128, 64, 32, 16, 8))
    return pl.pallas_call(
        _cross_kernel,
        grid=(b, l // bm),
        in_specs=[
            pl.BlockSpec((1, bm, d), lambda bi, i: (bi, i, 0)),
            pl.BlockSpec((1, d), lambda bi, i: (0, 0)),
            pl.BlockSpec((d, CX_W), lambda bi, i: (0, 0)),
            pl.BlockSpec((1, HEAD_DIM), lambda bi, i: (0, 0)),
            pl.BlockSpec((1, n_mem, CX_W), lambda bi, i: (bi, 0, 0)),
            pl.BlockSpec((1, n_mem, CX_W), lambda bi, i: (bi, 0, 0)),
            pl.BlockSpec((CX_W, d), lambda bi, i: (0, 0)),
        ],
        out_specs=pl.BlockSpec((1, bm, d), lambda bi, i: (bi, i, 0)),
        out_shape=jax.ShapeDtypeStruct((b, l, d), F32),
        scratch_shapes=[pltpu.VMEM((bm, d), BF16)],
        compiler_params=_cparams(("parallel", "arbitrary")),
        name="cross_attention",
    )(x, g_x.reshape(1, d), wq, q_gain.reshape(1, HEAD_DIM), kmem, vmem, wo)


def _pack_bf16_pairs(x):
    half = x.shape[1] // 2
    bits = lax.bitcast_convert_type(x.astype(BF16).astype(F32), jnp.uint32)
    return (bits[:, :half] & jnp.uint32(0xFFFF0000)) | (bits[:, half:] >> 16)


def _unpack_bf16_pairs(w):
    hi = lax.bitcast_convert_type(w & jnp.uint32(0xFFFF0000), F32)
    lo = lax.bitcast_convert_type(w << 16, F32)
    return hi, lo


def _router_kernel(x_ref, g_ref, rhi_ref, rlo_ref, idx_ref, w_ref, hp_ref, *, n_experts):
    hn = _rms_rows(x_ref[...], g_ref[...])
    hp_ref[...] = _pack_bf16_pairs(hn)
    hi = hn.astype(BF16)
    lo = (hn - hi.astype(F32)).astype(BF16)
    logits = (jnp.dot(hi, rhi_ref[...], preferred_element_type=F32)
              + jnp.dot(hi, rlo_ref[...], preferred_element_type=F32)
              + jnp.dot(lo, rhi_ref[...], preferred_element_type=F32))
    lane = lax.broadcasted_iota(jnp.int32, logits.shape, 1).astype(F32)
    low = jnp.float32(-3.0e38)
    l1 = jnp.where(lane < n_experts, logits, low)
    m1 = jnp.max(l1, axis=-1, keepdims=True)
    i1 = jnp.min(jnp.where(l1 == m1, lane, float(LANES)), axis=-1, keepdims=True)
    l2 = jnp.where(lane == i1, low, l1)
    m2 = jnp.max(l2, axis=-1, keepdims=True)
    i2 = jnp.min(jnp.where(l2 == m2, lane, float(LANES)), axis=-1, keepdims=True)
    e2 = jnp.exp(m2 - m1)
    w1 = 1.0 / (1.0 + e2)
    w2 = e2 / (1.0 + e2)
    idx_ref[...] = jnp.where(lane == 0.0, i1, jnp.where(lane == 1.0, i2, 0.0)).astype(jnp.int32)
    w_ref[...] = jnp.where(lane == 0.0, w1, jnp.where(lane == 1.0, w2, 0.0))


def route_tokens(x, gain, router):
    t, d = x.shape
    n_experts = router.shape[1]
    r = jnp.zeros((d, LANES), F32).at[:, :n_experts].set(router.astype(F32))
    r_hi = r.astype(BF16)
    r_lo = (r - r_hi.astype(F32)).astype(BF16)
    bm = _pick(t, (256, 128, 64, 32, 16, 8))
    return pl.pallas_call(
        functools.partial(_router_kernel, n_experts=n_experts),
        grid=(t // bm,),
        in_specs=[
            pl.BlockSpec((bm, d), lambda i: (i, 0)),
            pl.BlockSpec((1, d), lambda i: (0, 0)),
            pl.BlockSpec((d, LANES), lambda i: (0, 0)),
            pl.BlockSpec((d, LANES), lambda i: (0, 0)),
        ],
        out_specs=[pl.BlockSpec((bm, LANES), lambda i: (i, 0)), pl.BlockSpec((bm, LANES), lambda i: (i, 0)),
                   pl.BlockSpec((bm, d // 2), lambda i: (i, 0))],
        out_shape=[jax.ShapeDtypeStruct((t, LANES), jnp.int32), jax.ShapeDtypeStruct((t, LANES), F32),
                   jax.ShapeDtypeStruct((t, d // 2), jnp.uint32)],
        compiler_params=_cparams(("parallel",)),
        name="moe_router",
    )(x, gain.reshape(1, d), r_hi, r_lo)


def _swiglu_step(hn, wg_ref, wu_ref, wd_ref):
    a = jnp.dot(hn, wg_ref[0], preferred_element_type=F32)
    b = jnp.dot(hn, wu_ref[0], preferred_element_type=F32)
    hid = (a * jax.nn.sigmoid(a)) * b
    return jnp.dot(hid.astype(BF16), wd_ref[0], preferred_element_type=F32)


def _ffn_kernel(x_hbm, g_ref, wg_ref, wu_ref, wd_ref, o_ref, hn_ref, xbuf, sem, *, bm, n_blocks):
    i = pl.program_id(0)

    def fetch(blk):
        return pltpu.make_async_copy(x_hbm.at[pl.ds(pl.multiple_of(blk * bm, bm), bm)], xbuf, sem)

    @pl.when((i == 0) & (pl.program_id(1) == 0))
    def _():
        fetch(0).start()

    @pl.when(pl.program_id(1) == 0)
    def _():
        fetch(i).wait()
        _norm_block_to(xbuf, g_ref, hn_ref)
        o_ref[...] = xbuf[...]

        @pl.when(i + 1 < n_blocks)
        def _():
            fetch(i + 1).start()

    o_ref[...] += _swiglu_step(hn_ref[...], wg_ref, wu_ref, wd_ref)


FFN_TILE = 512
FFN_VMEM_LIMIT_BYTES = 60 * 1024 * 1024


def swiglu_ffn(x, gain, w_gu, w_down):
    t, d = x.shape
    ffn = w_down.shape[1]
    bm = _pick(t, (512, 256, 128, 64, 32, 16, 8))
    tf = _pick(ffn, (FFN_TILE, 256, 128))
    nf = ffn // tf
    n_blocks = t // bm
    return pl.pallas_call(
        functools.partial(_ffn_kernel, bm=bm, n_blocks=n_blocks),
        grid=(n_blocks, nf),
        in_specs=[
            pl.BlockSpec(memory_space=pl.ANY),
            pl.BlockSpec((1, d), lambda i, f: (0, 0)),
            pl.BlockSpec((1, d, tf), lambda i, f: (0, 0, f)),
            pl.BlockSpec((1, d, tf), lambda i, f: (0, 0, nf + f)),
            pl.BlockSpec((1, tf, d), lambda i, f: (0, f, 0)),
        ],
        out_specs=pl.BlockSpec((bm, d), lambda i, f: (i, 0)),
        out_shape=jax.ShapeDtypeStruct((t, d), F32),
        scratch_shapes=[pltpu.VMEM((bm, d), BF16), pltpu.VMEM((bm, d), F32), pltpu.SemaphoreType.DMA(())],
        compiler_params=_cparams(("arbitrary", "arbitrary"), FFN_VMEM_LIMIT_BYTES),
        name="dense_ffn",
    )(x, gain.reshape(1, d), w_gu, w_gu, w_down)


def _pad_ffn_weights(w_gu, w_down, tile):
    ffn = w_down.shape[0]
    pad = (-ffn) % tile
    if pad == 0:
        return w_gu, w_down
    wg = jnp.pad(w_gu[:, :ffn], ((0, 0), (0, pad)))
    wu = jnp.pad(w_gu[:, ffn:], ((0, 0), (0, pad)))
    return jnp.concatenate([wg, wu], axis=1), jnp.pad(w_down, ((0, pad), (0, 0)))


MOE_TILE_ROWS = 512
MOE_MOVE_TOKENS = 256
MOE_ZERO_ROWS = 128
MOE_ISSUE_UNROLL = 8


def _moe_plan(top_idx, n_e, bm, n_tiles):
    e_flat = top_idx[:, :TOP_K].reshape(-1)
    onehot = (e_flat[:, None] == jnp.arange(n_e, dtype=jnp.int32)[None, :]).astype(jnp.int32)
    csum = jnp.cumsum(onehot, axis=0)
    cnt = csum[-1]
    rank = jnp.sum(onehot * (csum - 1), axis=1)
    padded = ((cnt + bm - 1) // bm) * bm
    gend = jnp.cumsum(padded)
    gstart = gend - padded
    dest = (jnp.sum(onehot * gstart[None, :], axis=1) + rank).astype(jnp.int32)
    n_valid = (gend[-1] // bm).astype(jnp.int32)
    tile_start = jnp.arange(n_tiles, dtype=jnp.int32) * bm
    tile_e = jnp.sum((tile_start[:, None] >= gend[None, :]).astype(jnp.int32), axis=1)
    tile_e = jnp.minimum(tile_e, n_e - 1)
    last_e = jnp.sum(jnp.where(jnp.arange(n_tiles) == n_valid - 1, tile_e, 0))
    tile_e = jnp.where(jnp.arange(n_tiles) < n_valid, tile_e, last_e).astype(jnp.int32)
    return dest, tile_e, n_valid.reshape(1), gend.astype(jnp.int32), cnt.astype(jnp.int32)


def _dispatch_kernel(gend_ref, cnt_ref, nv_ref, dest_ref, x_ref, xs_hbm, zbuf, sem, zsem, *,
                     bt, bm, n_e, n_tiles):
    i = pl.program_id(0)

    @pl.when(i == 0)
    def _():
        zbuf[...] = jnp.zeros_like(zbuf)
        zr = zbuf.shape[0]

        def clear_tile(base):
            copies = [pltpu.make_async_copy(
                zbuf, xs_hbm.at[pl.ds(pl.multiple_of(base + z * zr, zr), zr)], zsem)
                for z in range(bm // zr)]
            for cp in copies:
                cp.start()
            for cp in copies:
                cp.wait()

        for e in range(n_e):
            @pl.when(cnt_ref[e] > 0)
            def _():
                clear_tile(gend_ref[e] - bm)

        for back in range(1, min(n_e, n_tiles) + 1):
            @pl.when(n_tiles - back >= nv_ref[0])
            def _():
                clear_tile((n_tiles - back) * bm)

    def issue(r, carry):
        for k in range(TOP_K):
            pltpu.make_async_copy(x_ref.at[pl.ds(r, 1)],
                                  xs_hbm.at[pl.ds(dest_ref[0, 0, TOP_K * r + k], 1)], sem).start()
        return carry

    lax.fori_loop(0, bt, issue, 0, unroll=MOE_ISSUE_UNROLL)
    for k in range(TOP_K):
        pltpu.make_async_copy(x_ref, xs_hbm.at[pl.ds(0, bt)], sem).wait()


def moe_dispatch(x, dest, gend, cnt, n_valid, n_tiles, bm):
    t, d = x.shape
    n_rows = n_tiles * bm
    bt = _pick(t, (MOE_MOVE_TOKENS, 64, 32, 16, 8))
    n_steps = t // bt
    zr = min(MOE_ZERO_ROWS, bm)
    return pl.pallas_call(
        functools.partial(_dispatch_kernel, bt=bt, bm=bm, n_e=cnt.shape[0], n_tiles=n_tiles),
        grid_spec=pltpu.PrefetchScalarGridSpec(
            num_scalar_prefetch=3,
            grid=(n_steps,),
            in_specs=[
                pl.BlockSpec((1, 1, TOP_K * bt), lambda i, ge, cn, nv: (i, 0, 0), memory_space=pltpu.SMEM),
                pl.BlockSpec((bt, d), lambda i, ge, cn, nv: (i, 0)),
            ],
            out_specs=pl.BlockSpec(memory_space=pl.ANY),
            scratch_shapes=[pltpu.VMEM((zr, d), x.dtype), pltpu.SemaphoreType.DMA(()),
                            pltpu.SemaphoreType.DMA(())],
        ),
        out_shape=jax.ShapeDtypeStruct((n_rows, d), x.dtype),
        compiler_params=_cparams(("arbitrary",)),
        name="moe_dispatch",
    )(gend, cnt, n_valid, dest.reshape(n_steps, 1, TOP_K * bt), x)


def _moe_ffn_kernel(te_ref, nv_ref, x_ref, wg_ref, wu_ref, wd_ref, o_ref, hn_ref, acc_ref, *, nf, slab):
    j = pl.program_id(0)
    f = pl.program_id(1)
    valid = j < nv_ref[0]
    rows, half = x_ref.shape

    @pl.when(valid & (f == 0))
    def _():
        def unpack(r, carry):
            r0 = pl.multiple_of(r * slab, slab)
            hi, lo = _unpack_bf16_pairs(x_ref[pl.ds(r0, slab), :])
            hn_ref[pl.ds(r0, slab), pl.ds(0, half)] = hi.astype(hn_ref.dtype)
            hn_ref[pl.ds(r0, slab), pl.ds(half, half)] = lo.astype(hn_ref.dtype)
            return carry

        lax.fori_loop(0, rows // slab, unpack, 0)
        acc_ref[...] = _swiglu_step(hn_ref[...], wg_ref, wu_ref, wd_ref)

    @pl.when(valid & (f > 0))
    def _():
        acc_ref[...] += _swiglu_step(hn_ref[...], wg_ref, wu_ref, wd_ref)

    @pl.when(valid & (f == nf - 1))
    def _():
        def pack(r, carry):
            r0 = pl.multiple_of(r * slab, slab)
            o_ref[pl.ds(r0, slab), :] = _pack_bf16_pairs(acc_ref[pl.ds(r0, slab), :])
            return carry

        lax.fori_loop(0, rows // slab, pack, 0)

    @pl.when(jnp.logical_not(valid) & (f == nf - 1))
    def _():
        o_ref[...] = jnp.zeros_like(o_ref)


def moe_expert_ffn(xs, w_gu, w_down, tile_e, n_valid, bm):
    n_rows, half = xs.shape
    d = 2 * half
    n_tiles = n_rows // bm
    slab = min(64, bm)
    ffn = w_down.shape[1]
    tf = _pick(ffn, (256, 128))
    nf = ffn // tf

    def fidx(j, f, nv):
        return jnp.where(j < nv[0], f, nf - 1)

    return pl.pallas_call(
        functools.partial(_moe_ffn_kernel, nf=nf, slab=slab),
        grid_spec=pltpu.PrefetchScalarGridSpec(
            num_scalar_prefetch=2,
            grid=(n_tiles, nf),
            in_specs=[
                pl.BlockSpec((bm, half), lambda j, f, te, nv: (jnp.minimum(j, nv[0] - 1), 0)),
                pl.BlockSpec((1, d, tf), lambda j, f, te, nv: (te[j], 0, fidx(j, f, nv))),
                pl.BlockSpec((1, d, tf), lambda j, f, te, nv: (te[j], 0, nf + fidx(j, f, nv))),
                pl.BlockSpec((1, tf, d), lambda j, f, te, nv: (te[j], fidx(j, f, nv), 0)),
            ],
            out_specs=pl.BlockSpec((bm, half), lambda j, f, te, nv: (j, 0)),
            scratch_shapes=[pltpu.VMEM((bm, d), BF16), pltpu.VMEM((bm, d), F32)],
        ),
        out_shape=jax.ShapeDtypeStruct((n_rows, half), jnp.uint32),
        compiler_params=_cparams(("arbitrary", "arbitrary")),
        name="moe_ffn",
    )(tile_e, n_valid, xs, w_gu, w_gu, w_down)


def _combine_kernel(pos_ref, posn_ref, x_ref, w_ref, y_hbm, o_ref, ybuf, sems, *, bt, n_steps):
    i = pl.program_id(0)
    slot = i % 2

    def issue(p_ref, s):
        def body(r, carry):
            for k in range(TOP_K):
                pltpu.make_async_copy(y_hbm.at[pl.ds(p_ref[0, 0, TOP_K * r + k], 1)],
                                      ybuf.at[s, k, pl.ds(r, 1)], sems.at[s]).start()
            return carry

        lax.fori_loop(0, bt, body, 0, unroll=MOE_ISSUE_UNROLL)

    @pl.when(i == 0)
    def _():
        issue(pos_ref, 0)

    @pl.when(i + 1 < n_steps)
    def _():
        issue(posn_ref, 1 - slot)

    for k in range(TOP_K):
        pltpu.make_async_copy(y_hbm.at[pl.ds(0, bt)], ybuf.at[slot, k], sems.at[slot]).wait()
    w = w_ref[...]
    half = ybuf.shape[-1]
    a_hi, a_lo = _unpack_bf16_pairs(ybuf[slot, 0])
    b_hi, b_lo = _unpack_bf16_pairs(ybuf[slot, 1])
    o_ref[:, :half] = x_ref[:, :half] + w[:, 0:1] * a_hi + w[:, 1:2] * b_hi
    o_ref[:, half:] = x_ref[:, half:] + w[:, 0:1] * a_lo + w[:, 1:2] * b_lo


def moe_combine(x, top_w, dest, ys):
    t, d = x.shape
    bt = _pick(t, (MOE_MOVE_TOKENS, 64, 32, 16, 8))
    n_steps = t // bt
    pos = dest.reshape(n_steps, 1, TOP_K * bt)
    return pl.pallas_call(
        functools.partial(_combine_kernel, bt=bt, n_steps=n_steps),
        grid=(n_steps,),
        in_specs=[
            pl.BlockSpec((1, 1, TOP_K * bt), lambda i: (i, 0, 0), memory_space=pltpu.SMEM),
            pl.BlockSpec((1, 1, TOP_K * bt), lambda i: (jnp.minimum(i + 1, n_steps - 1), 0, 0),
                         memory_space=pltpu.SMEM),
            pl.BlockSpec((bt, d), lambda i: (i, 0)),
            pl.BlockSpec((bt, LANES), lambda i: (i, 0)),
            pl.BlockSpec(memory_space=pl.ANY),
        ],
        out_specs=pl.BlockSpec((bt, d), lambda i: (i, 0)),
        out_shape=jax.ShapeDtypeStruct((t, d), F32),
        scratch_shapes=[pltpu.VMEM((2, TOP_K, bt, d // 2), jnp.uint32), pltpu.SemaphoreType.DMA((2,))],
        compiler_params=_cparams(("arbitrary",)),
        name="moe_combine",
    )(pos, pos, x, top_w, ys)


def moe_ffn(x, gain, router, w_gu, w_down):
    t, d = x.shape
    n_e = w_down.shape[0]
    bm = min(MOE_TILE_ROWS, max(8, t // 8))
    n_tiles = (TOP_K * t + n_e * (bm - 1) + bm - 1) // bm
    top_idx, top_w, hn_packed = route_tokens(x, gain, router)
    dest, tile_e, n_valid, gend, cnt = _moe_plan(top_idx, n_e, bm, n_tiles)
    xs = moe_dispatch(hn_packed, dest, gend, cnt, n_valid, n_tiles, bm)
    ys = moe_expert_ffn(xs, w_gu, w_down, tile_e, n_valid, bm)
    return moe_combine(x, top_w, dest, ys)


def _prep_layer(i, p):
    w_in = p["w_in"][i]
    d = w_in.shape[0]
    w_main = jnp.concatenate([w_in[:, :OFF_GATE_END], w_in[:, OFF_DECAY_END:]], axis=1).astype(BF16)
    w_decay = jnp.zeros((d, LANES), BF16).at[:, :2 * GLA_RANK].set(
        w_in[:, OFF_GATE_END:OFF_DECAY_END].astype(BF16))

    def up_mat(up, row0):
        u = up.reshape(GLA_RANK, GLA_HEADS, GLA_DK).transpose(1, 0, 2).astype(BF16)
        return jnp.zeros((GLA_HEADS, LANES, GLA_DK), BF16).at[:, row0:row0 + GLA_RANK, :].set(u)

    return dict(
        norm_mix=p["norm_mix"][i], w_main=w_main, w_decay=w_decay,
        u_f=up_mat(p["gla_up_f"][i], 0), u_b=up_mat(p["gla_up_b"][i], GLA_RANK),
        bias_f=p["gla_bias_f"][i].reshape(GLA_HEADS, 1, GLA_DK).astype(F32),
        bias_b=p["gla_bias_b"][i].reshape(GLA_HEADS, 1, GLA_DK).astype(F32),
        gla_out_norm=p["gla_out_norm"][i], swa_q_norm=p["swa_q_norm"][i], swa_k_norm=p["swa_k_norm"][i],
        swa_sink=p["swa_sink"][i], w_out=p["w_out"][i].astype(BF16),
        norm_cross=p["norm_cross"][i], norm_mem=p["norm_mem"][i],
        cx_wq=p["cx_wq"][i].astype(BF16), cx_wkv=p["cx_wkv"][i].astype(BF16),
        cx_q_norm=p["cx_q_norm"][i], cx_k_norm=p["cx_k_norm"][i], cx_wo=p["cx_wo"][i].astype(BF16),
        norm_ffn=p["norm_ffn"][i],
    )


def _trunk(x, mem, layers, bias_tab):
    b, l, d = x.shape
    t = b * l
    for i, lw in enumerate(layers):
        proj, decay_in = in_proj(x.reshape(t, d), lw["norm_mix"], lw["w_main"], lw["w_decay"])
        proj = proj.reshape(b, l, MAIN_W)
        decay_in = decay_in.reshape(b, l, LANES)
        o_f = gla_direction(proj, decay_in, lw["u_f"], lw["bias_f"], reverse=False)
        o_gla = gla_direction(proj, decay_in, lw["u_b"], lw["bias_b"], reverse=True,
                              o_fwd=o_f, out_gain=lw["gla_out_norm"])
        o_swa = swa_mixer(proj, bias_tab, lw["swa_sink"], lw["swa_q_norm"], lw["swa_k_norm"])
        x1 = out_proj(o_gla.reshape(t, GLA_V_W), o_swa.reshape(t, SWA_Q_W), lw["w_out"], x.reshape(t, d))
        kmem, vmem = cross_kv(mem, lw["norm_mem"], lw["cx_wkv"], lw["cx_k_norm"])
        x2 = cross_attention(x1.reshape(b, l, d), kmem, vmem, lw["norm_cross"], lw["cx_wq"],
                             lw["cx_q_norm"], lw["cx_wo"]).reshape(t, d)
        if i % 2 == 0:
            x3 = swiglu_ffn(x2, lw["norm_ffn"], lw["ffn_w_gu"], lw["ffn_w_down"])
        else:
            x3 = moe_ffn(x2, lw["norm_ffn"], lw["moe_router"], lw["moe_w_gu"], lw["moe_w_down"])
        x = x3.reshape(b, l, d)
    return x


def kernel(x_prompt, x_sample, mem_prompt, mem_sample, rel_bias, norm_mix, w_in, gla_up_f, gla_bias_f, gla_up_b, gla_bias_b, gla_out_norm, swa_q_norm, swa_k_norm, swa_sink, w_out, norm_cross, norm_mem, cx_wq, cx_wkv, cx_q_norm, cx_k_norm, cx_wo, norm_ffn, ffn_w_gu, ffn_w_down, moe_router, moe_w_gu, moe_w_down):
    p = dict(norm_mix=norm_mix, w_in=w_in, gla_up_f=gla_up_f, gla_bias_f=gla_bias_f, gla_up_b=gla_up_b,
             gla_bias_b=gla_bias_b, gla_out_norm=gla_out_norm, swa_q_norm=swa_q_norm, swa_k_norm=swa_k_norm,
             swa_sink=swa_sink, w_out=w_out, norm_cross=norm_cross, norm_mem=norm_mem, cx_wq=cx_wq,
             cx_wkv=cx_wkv, cx_q_norm=cx_q_norm, cx_k_norm=cx_k_norm, cx_wo=cx_wo, norm_ffn=norm_ffn)
    depth = w_in.shape[0]
    layers = []
    for i in range(depth):
        lw = _prep_layer(i, p)
        if i % 2 == 0:
            w_gu_p, w_down_p = _pad_ffn_weights(ffn_w_gu[i // 2].astype(BF16),
                                                ffn_w_down[i // 2].astype(BF16), FFN_TILE)
            lw["ffn_w_gu"] = w_gu_p[None]
            lw["ffn_w_down"] = w_down_p[None]
        else:
            lw["moe_router"] = moe_router[i // 2]
            lw["moe_w_gu"] = moe_w_gu[i // 2].astype(BF16)
            lw["moe_w_down"] = moe_w_down[i // 2].astype(BF16)
        layers.append(lw)
    bias_tab = swa_bias_table(rel_bias)
    y_prompt = _trunk(x_prompt, mem_prompt, layers, bias_tab)
    y_sample = _trunk(x_sample, mem_sample, layers, bias_tab)
    return (y_prompt, y_sample)
```

```python
import functools

import numpy as np
import jax
import jax.numpy as jnp
from jax import lax
from jax.experimental import pallas as pl
from jax.experimental.pallas import tpu as pltpu

F32 = jnp.float32
BF16 = jnp.bfloat16

HEAD_DIM = 128
GLA_HEADS = 8
GLA_DK = 128
GLA_DV = 256
GLA_RANK = 16
GLA_TAU = 16.0
GLA_CHUNK = 64
GLA_HEADS_PER_STEP = 4
SWA_HQ = 16
SWA_HKV = 4
SWA_GROUP = SWA_HQ // SWA_HKV
WINDOW = 128
N_BUCKETS = 32
MAX_DISTANCE = 128
CX_HEADS = 4
TOP_K = 2
EPS = 1e-6
NEG_INF = -1e30

GLA_QK_W = GLA_HEADS * GLA_DK
GLA_V_W = GLA_HEADS * GLA_DV
SWA_Q_W = SWA_HQ * HEAD_DIM
SWA_KV_W = SWA_HKV * HEAD_DIM
CX_W = CX_HEADS * HEAD_DIM
OFF_GATE_END = 2 * GLA_QK_W + 2 * GLA_V_W
OFF_DECAY_END = OFF_GATE_END + 2 * GLA_RANK
MAIN_W = OFF_GATE_END + SWA_Q_W + 2 * SWA_KV_W

LANES = 128
VMEM_LIMIT_BYTES = 56 * 1024 * 1024


def _cparams(sem, vmem_limit_bytes=VMEM_LIMIT_BYTES):
    return pltpu.CompilerParams(dimension_semantics=sem, vmem_limit_bytes=vmem_limit_bytes)


def _rms_rows(x, gain):
    ms = jnp.mean(x * x, axis=-1, keepdims=True)
    return x * lax.rsqrt(ms + EPS) * gain


def _norm_block_to(x_ref, g_ref, hn_ref, slab=16):
    rows = x_ref.shape[0]
    slab = min(slab, rows)

    def body(r, carry):
        r0 = pl.multiple_of(r * slab, slab)
        hn_ref[pl.ds(r0, slab), :] = _rms_rows(x_ref[pl.ds(r0, slab), :], g_ref[...]).astype(hn_ref.dtype)
        return carry

    lax.fori_loop(0, rows // slab, body, 0, unroll=8)


def _pick(n, cands):
    for c in cands:
        if n % c == 0:
            return c
    return n


def _in_proj_kernel(x_ref, g_ref, w_ref, wa_ref, o_ref, oa_ref, hn_ref):
    @pl.when(pl.program_id(1) == 0)
    def _():
        _norm_block_to(x_ref, g_ref, hn_ref)
        oa_ref[...] = jnp.dot(hn_ref[...], wa_ref[...], preferred_element_type=F32)

    o_ref[...] = jnp.dot(hn_ref[...], w_ref[...], preferred_element_type=F32).astype(o_ref.dtype)


def in_proj(x, gain, w_main, w_decay):
    t, d = x.shape
    n = w_main.shape[1]
    bm = _pick(t, (512, 256, 128, 64, 32, 16, 8))
    bn = _pick(n, (1536, 1024, 768, 512, 256, 128))
    return pl.pallas_call(
        _in_proj_kernel,
        grid=(t // bm, n // bn),
        in_specs=[
            pl.BlockSpec((bm, d), lambda i, j: (i, 0)),
            pl.BlockSpec((1, d), lambda i, j: (0, 0)),
            pl.BlockSpec((d, bn), lambda i, j: (0, j)),
            pl.BlockSpec((d, LANES), lambda i, j: (0, 0)),
        ],
        out_specs=[
            pl.BlockSpec((bm, bn), lambda i, j: (i, j)),
            pl.BlockSpec((bm, LANES), lambda i, j: (i, 0)),
        ],
        out_shape=[jax.ShapeDtypeStruct((t, n), BF16), jax.ShapeDtypeStruct((t, LANES), F32)],
        scratch_shapes=[pltpu.VMEM((bm, d), BF16)],
        compiler_params=_cparams(("parallel", "arbitrary")),
        name="in_proj",
    )(x, gain.reshape(1, d), w_main, w_decay)


def _log_sigmoid(z):
    return jnp.minimum(z, 0.0) - jnp.log1p(jnp.exp(-jnp.abs(z)))


def _gla_kernel(*refs, reverse, finalize, rows):
    if finalize:
        q_ref, k_ref, v_ref, a_ref, u_ref, b_ref, tri_ref, msk_ref, of_ref, g_ref, gain_ref, o_ref, st_ref = refs
    else:
        q_ref, k_ref, v_ref, a_ref, u_ref, b_ref, tri_ref, msk_ref, o_ref, st_ref = refs
    c = GLA_CHUNK
    nch = rows // c
    nh = st_ref.shape[0]
    hk = [slice(i * GLA_DK, (i + 1) * GLA_DK) for i in range(nh)]
    hv = [slice(i * GLA_DV, (i + 1) * GLA_DV) for i in range(nh)]

    @pl.when(pl.program_id(2) == 0)
    def _():
        st_ref[...] = jnp.zeros_like(st_ref)

    scale = GLA_DK ** -0.5
    nt = (((1,), (1,)), ((), ()))
    tn = (((0,), (0,)), ((), ()))

    a = a_ref[0].astype(BF16)
    zs = [jnp.dot(a, u_ref[i], preferred_element_type=F32) + b_ref[i] for i in range(nh)]
    las = [_log_sigmoid(z) * (1.0 / GLA_TAU) for z in zs]

    hls = []
    for la in las:
        la_hi = la.astype(BF16)
        la_lo = (la - la_hi.astype(F32)).astype(BF16)
        hls.append(jnp.concatenate([la_hi, la_lo], axis=1))
    cum2s = [jnp.dot(tri_ref[...], hl, preferred_element_type=F32) for hl in hls]

    q_ins, k_ins, k_ends, lasts_all = [], [], [], []
    for i, cum2 in enumerate(cum2s):
        cum = cum2[:, :GLA_DK] + cum2[:, GLA_DK:]
        lasts = [cum[ch * c:ch * c + 1] if reverse else cum[(ch + 1) * c - 1:(ch + 1) * c]
                 for ch in range(nch)]
        last_rows = jnp.concatenate([jnp.broadcast_to(l, (c, GLA_DK)) for l in lasts], axis=0)
        q = q_ref[0, :, hk[i]].astype(F32)
        k = k_ref[0, :, hk[i]].astype(F32)
        q_ins.append(((q * scale) * jnp.exp(cum)).astype(BF16))
        k_ins.append((k * jnp.exp(-cum)).astype(BF16))
        k_ends.append((k * jnp.exp(last_rows - cum)).astype(BF16))
        lasts_all.append(lasts)

    scs = [lax.dot_general(q_in, k_in, nt, preferred_element_type=F32) for q_in, k_in in zip(q_ins, k_ins)]
    scs = [jnp.where(msk_ref[...] > 0.5, sc, 0.0).astype(BF16) for sc in scs]
    o_intras = [jnp.dot(sc, v_ref[0, :, hv[i]], preferred_element_type=F32) for i, sc in enumerate(scs)]
    kvs_all = [[lax.dot_general(v_ref[0, ch * c:(ch + 1) * c, hv[i]], k_ends[i][ch * c:(ch + 1) * c], tn,
                                preferred_element_type=F32) for ch in range(nch)]
               for i in range(nh)]

    order = range(nch - 1, -1, -1) if reverse else range(nch)
    for i in range(nh):
        st = st_ref[i]
        for ch in order:
            sl = slice(ch * c, (ch + 1) * c)
            o = o_intras[i][sl] + lax.dot_general(q_ins[i][sl], st.astype(BF16), nt,
                                                  preferred_element_type=F32)
            st = st * jnp.exp(lasts_all[i][ch]) + kvs_all[i][ch]
            if finalize:
                o = o + of_ref[0, sl, hv[i]]
                o = _rms_rows(o, gain_ref[...])
                g = g_ref[0, sl, hv[i]].astype(F32)
                o_ref[0, sl, hv[i]] = (o * (g * jax.nn.sigmoid(g))).astype(o_ref.dtype)
            else:
                o_ref[0, sl, hv[i]] = o
        st_ref[i] = st


def gla_direction(proj, decay_in, u, bias, *, reverse, o_fwd=None, out_gain=None):
    b, l, _ = proj.shape
    rows = _pick(l, (512, 256, 128, 64))
    nb = l // rows
    finalize = o_fwd is not None
    blk = (lambda s: nb - 1 - s) if reverse else (lambda s: s)
    nh = GLA_HEADS_PER_STEP
    wk, wv = nh * GLA_DK, nh * GLA_DV
    kq = GLA_QK_W // wk
    kv = 2 * GLA_QK_W // wv
    kg = kv + GLA_HEADS // nh
    in_specs = [
        pl.BlockSpec((1, rows, wk), lambda bi, h, s: (bi, blk(s), h)),
        pl.BlockSpec((1, rows, wk), lambda bi, h, s: (bi, blk(s), kq + h)),
        pl.BlockSpec((1, rows, wv), lambda bi, h, s: (bi, blk(s), kv + h)),
        pl.BlockSpec((1, rows, LANES), lambda bi, h, s: (bi, blk(s), 0)),
        pl.BlockSpec((nh, LANES, GLA_DK), lambda bi, h, s: (h, 0, 0)),
        pl.BlockSpec((nh, 1, GLA_DK), lambda bi, h, s: (h, 0, 0)),
        pl.BlockSpec((rows, rows), lambda bi, h, s: (0, 0)),
        pl.BlockSpec((rows, rows), lambda bi, h, s: (0, 0)),
    ]
    ri = np.arange(rows)[:, None]
    ci = np.arange(rows)[None, :]
    same = (ri // GLA_CHUNK) == (ci // GLA_CHUNK)
    tri = same & ((ci >= ri) if reverse else (ci <= ri))
    msk = same & ((ci > ri) if reverse else (ci <= ri))
    args = [proj, proj, proj, decay_in, u, bias, jnp.asarray(tri, BF16), jnp.asarray(msk, F32)]
    if finalize:
        in_specs += [
            pl.BlockSpec((1, rows, wv), lambda bi, h, s: (bi, blk(s), h)),
            pl.BlockSpec((1, rows, wv), lambda bi, h, s: (bi, blk(s), kg + h)),
            pl.BlockSpec((1, GLA_DV), lambda bi, h, s: (0, 0)),
        ]
        args += [o_fwd, proj, out_gain.reshape(1, GLA_DV)]
    out_dtype = BF16 if finalize else F32
    return pl.pallas_call(
        functools.partial(_gla_kernel, reverse=reverse, finalize=finalize, rows=rows),
        grid=(b, GLA_HEADS // nh, nb),
        in_specs=in_specs,
        out_specs=pl.BlockSpec((1, rows, wv), lambda bi, h, s: (bi, blk(s), h)),
        out_shape=jax.ShapeDtypeStruct((b, l, GLA_V_W), out_dtype),
        scratch_shapes=[pltpu.VMEM((nh, GLA_DV, GLA_DK), F32)],
        compiler_params=_cparams(("parallel", "parallel", "arbitrary")),
        name="gla_bwd" if reverse else "gla_fwd",
    )(*args)


def _t5_buckets(rel):
    half = N_BUCKETS // 2
    ret = (rel > 0).astype(np.int32) * half
    n = np.abs(rel)
    max_exact = half // 2
    large = max_exact + (np.log(np.maximum(n, 1) / max_exact) / np.log(MAX_DISTANCE / max_exact)
                         * (half - max_exact)).astype(np.int32)
    large = np.minimum(large, half - 1)
    return ret + np.where(n < max_exact, n, large)


def _bias_table_kernel(rb_ref, bkt_ref, o_ref):
    h = pl.program_id(0)
    bkt = bkt_ref[...]
    for g in range(SWA_GROUP):
        acc = jnp.zeros(bkt.shape, F32)
        for b in range(N_BUCKETS):
            acc = jnp.where(bkt == b, rb_ref[b, h * SWA_GROUP + g], acc)
        o_ref[0, g * WINDOW:(g + 1) * WINDOW, :] = acc


def swa_bias_table(rel_bias):
    rel = np.arange(3 * WINDOW)[None, :] - WINDOW - np.arange(WINDOW)[:, None]
    bkt = jnp.asarray(_t5_buckets(rel), jnp.int32)
    return pl.pallas_call(
        _bias_table_kernel,
        grid=(SWA_HKV,),
        in_specs=[
            pl.BlockSpec(memory_space=pltpu.SMEM),
            pl.BlockSpec((WINDOW, 3 * WINDOW), lambda h: (0, 0)),
        ],
        out_specs=pl.BlockSpec((1, SWA_GROUP * WINDOW, 3 * WINDOW), lambda h: (h, 0, 0)),
        out_shape=jax.ShapeDtypeStruct((SWA_HKV, SWA_GROUP * WINDOW, 3 * WINDOW), F32),
        compiler_params=_cparams(("parallel",)),
        name="swa_bias_table",
    )(rel_bias.astype(F32), bkt)


def _swa_kernel(sink_ref, q_ref, kp_ref, kc_ref, kn_ref, vp_ref, vc_ref, vn_ref, bias_ref,
                qg_ref, kg_ref, o_ref, *, nb):
    i = pl.program_id(1)
    w = WINDOW
    gw = SWA_GROUP * HEAD_DIM
    row = lax.broadcasted_iota(jnp.int32, (w, 3 * w), 0)
    col = lax.broadcasted_iota(jnp.int32, (w, 3 * w), 1)
    rel = col - w - row
    valid = (jnp.abs(rel) <= w) & ((col >= w) | (i > 0)) & ((col < 2 * w) | (i < nb - 1))
    heads = [(h, g) for h in range(SWA_HKV) for g in range(SWA_GROUP)]
    kns, vs = [], []
    for h in range(SWA_HKV):
        hs = slice(h * HEAD_DIM, (h + 1) * HEAD_DIM)
        k = jnp.concatenate([kp_ref[0, :, hs], kc_ref[0, :, hs], kn_ref[0, :, hs]], axis=0).astype(F32)
        kns.append(_rms_rows(k, kg_ref[...]).astype(BF16))
        vs.append(jnp.concatenate([vp_ref[0, :, hs], vc_ref[0, :, hs], vn_ref[0, :, hs]], axis=0))
    scores = []
    for h, g in heads:
        qs = slice(h * gw + g * HEAD_DIM, h * gw + (g + 1) * HEAD_DIM)
        qn = _rms_rows(q_ref[0, :, qs].astype(F32), qg_ref[...]).astype(BF16)
        scores.append(lax.dot_general(qn, kns[h], (((1,), (1,)), ((), ())), preferred_element_type=F32))
    probs = []
    for (h, g), s in zip(heads, scores):
        logits = s * (HEAD_DIM ** -0.5) + bias_ref[h, g * w:(g + 1) * w, :]
        logits = jnp.where(valid, logits, NEG_INF)
        sink = sink_ref[h * SWA_GROUP + g]
        m = jnp.maximum(jnp.max(logits, axis=-1, keepdims=True), sink)
        p = jnp.exp(logits - m)
        denom = jnp.sum(p, axis=-1, keepdims=True) + jnp.exp(sink - m)
        probs.append((p.astype(BF16), denom))
    for (h, g), (p, denom) in zip(heads, probs):
        qs = slice(h * gw + g * HEAD_DIM, h * gw + (g + 1) * HEAD_DIM)
        o = jnp.dot(p, vs[h], preferred_element_type=F32) / denom
        o_ref[0, :, qs] = o.astype(o_ref.dtype)


def swa_mixer(proj, bias_tab, sink, q_gain, k_gain):
    b, l, _ = proj.shape
    nb = l // WINDOW
    q0 = OFF_GATE_END // SWA_Q_W
    k0 = (OFF_GATE_END + SWA_Q_W) // SWA_KV_W
    v0 = k0 + 1
    assert OFF_GATE_END % SWA_Q_W == 0 and (OFF_GATE_END + SWA_Q_W) % SWA_KV_W == 0

    def kv_spec(base, shift):
        return pl.BlockSpec(
            (1, WINDOW, SWA_KV_W),
            lambda bi, i: (bi, jnp.clip(i + shift, 0, nb - 1), base))

    return pl.pallas_call(
        functools.partial(_swa_kernel, nb=nb),
        grid=(b, nb),
        in_specs=[
            pl.BlockSpec(memory_space=pltpu.SMEM),
            pl.BlockSpec((1, WINDOW, SWA_Q_W), lambda bi, i: (bi, i, q0)),
            kv_spec(k0, -1), kv_spec(k0, 0), kv_spec(k0, 1),
            kv_spec(v0, -1), kv_spec(v0, 0), kv_spec(v0, 1),
            pl.BlockSpec((SWA_HKV, SWA_GROUP * WINDOW, 3 * WINDOW), lambda bi, i: (0, 0, 0)),
            pl.BlockSpec((1, HEAD_DIM), lambda bi, i: (0, 0)),
            pl.BlockSpec((1, HEAD_DIM), lambda bi, i: (0, 0)),
        ],
        out_specs=pl.BlockSpec((1, WINDOW, SWA_Q_W), lambda bi, i: (bi, i, 0)),
        out_shape=jax.ShapeDtypeStruct((b, l, SWA_Q_W), BF16),
        compiler_params=_cparams(("parallel", "arbitrary")),
        name="swa",
    )(sink.astype(F32), proj, proj, proj, proj, proj, proj, proj, bias_tab,
      q_gain.reshape(1, HEAD_DIM), k_gain.reshape(1, HEAD_DIM))


def _out_proj_kernel(a_ref, b_ref, wa_ref, wb_ref, x_ref, o_ref):
    o_ref[...] = (x_ref[...]
                  + jnp.dot(a_ref[...], wa_ref[...], preferred_element_type=F32)
                  + jnp.dot(b_ref[...], wb_ref[...], preferred_element_type=F32))


def out_proj(o_gla, o_swa, w_out, x):
    t, d = x.shape
    ka, kb = o_gla.shape[1], o_swa.shape[1]
    bm = _pick(t, (1024, 512, 256, 128, 64, 32, 16, 8))
    bn = _pick(d, (1024, 512, 256, 128))
    assert ka == kb and w_out.shape[0] == ka + kb
    return pl.pallas_call(
        _out_proj_kernel,
        grid=(t // bm, d // bn),
        in_specs=[
            pl.BlockSpec((bm, ka), lambda i, j: (i, 0)),
            pl.BlockSpec((bm, kb), lambda i, j: (i, 0)),
            pl.BlockSpec((ka, bn), lambda i, j: (0, j)),
            pl.BlockSpec((kb, bn), lambda i, j: (1, j)),
            pl.BlockSpec((bm, bn), lambda i, j: (i, j)),
        ],
        out_specs=pl.BlockSpec((bm, bn), lambda i, j: (i, j)),
        out_shape=jax.ShapeDtypeStruct((t, d), F32),
        compiler_params=_cparams(("parallel", "arbitrary")),
        name="out_proj",
    )(o_gla, o_swa, w_out, w_out, x)


def _cross_kv_kernel(m_ref, g_ref, w_ref, kg_ref, k_ref, v_ref, hn_ref):
    _norm_block_to(m_ref.at[0], g_ref, hn_ref)
    kv = jnp.dot(hn_ref[...], w_ref[...], preferred_element_type=F32)
    for h in range(CX_HEADS):
        kh = kv[:, h * HEAD_DIM:(h + 1) * HEAD_DIM]
        k_ref[0, :, h * HEAD_DIM:(h + 1) * HEAD_DIM] = _rms_rows(kh, kg_ref[...]).astype(k_ref.dtype)
    v_ref[0] = kv[:, CX_W:].astype(v_ref.dtype)


def cross_kv(mem, g_m, wkv, k_gain):
    b, n_mem, d = mem.shape
    return pl.pallas_call(
        _cross_kv_kernel,
        grid=(b,),
        in_specs=[
            pl.BlockSpec((1, n_mem, d), lambda i: (i, 0, 0)),
            pl.BlockSpec((1, d), lambda i: (0, 0)),
            pl.BlockSpec((d, 2 * CX_W), lambda i: (0, 0)),
            pl.BlockSpec((1, HEAD_DIM), lambda i: (0, 0)),
        ],
        out_specs=[
            pl.BlockSpec((1, n_mem, CX_W), lambda i: (i, 0, 0)),
            pl.BlockSpec((1, n_mem, CX_W), lambda i: (i, 0, 0)),
        ],
        out_shape=[jax.ShapeDtypeStruct((b, n_mem, CX_W), BF16)] * 2,
        scratch_shapes=[pltpu.VMEM((n_mem, d), BF16)],
        compiler_params=_cparams(("parallel",)),
        name="cross_kv",
    )(mem, g_m.reshape(1, d), wkv, k_gain.reshape(1, HEAD_DIM))


def _cross_kernel(x_ref, g_ref, wq_ref, qg_ref, k_ref, v_ref, wo_ref, o_ref, hn_ref):
    _norm_block_to(x_ref.at[0], g_ref, hn_ref)
    q = jnp.dot(hn_ref[...], wq_ref[...], preferred_element_type=F32)
    hsl = [slice(h * HEAD_DIM, (h + 1) * HEAD_DIM) for h in range(CX_HEADS)]
    qhs = [_rms_rows(q[:, hs], qg_ref[...]).astype(BF16) for hs in hsl]
    ss = [lax.dot_general(qh, k_ref[0, :, hs], (((1,), (1,)), ((), ())), preferred_element_type=F32)
          for qh, hs in zip(qhs, hsl)]
    ps = []
    for s in ss:
        s = s * (HEAD_DIM ** -0.5)
        m = jnp.max(s, axis=-1, keepdims=True)
        p = jnp.exp(s - m)
        ps.append((p.astype(BF16), jnp.sum(p, axis=-1, keepdims=True)))
    outs = [jnp.dot(p, v_ref[0, :, hs], preferred_element_type=F32) / denom
            for (p, denom), hs in zip(ps, hsl)]
    o = jnp.concatenate(outs, axis=1).astype(BF16)
    o_ref[0] = x_ref[0] + jnp.dot(o, wo_ref[...], preferred_element_type=F32)


def cross_attention(x, kmem, vmem, g_x, wq, q_gain, wo):
    b, l, d = x.shape
    n_mem = kmem.shape[1]
    bm = _pick(l, (512, 256, 128, 64, 32, 16, 8))
    return pl.pallas_call(
        _cross_kernel,
        grid=(b, l // bm),
        in_specs=[
            pl.BlockSpec((1, bm, d), lambda bi, i: (bi, i, 0)),
            pl.BlockSpec((1, d), lambda bi, i: (0, 0)),
            pl.BlockSpec((d, CX_W), lambda bi, i: (0, 0)),
            pl.BlockSpec((1, HEAD_DIM), lambda bi, i: (0, 0)),
            pl.BlockSpec((1, n_mem, CX_W), lambda bi, i: (bi, 0, 0)),
            pl.BlockSpec((1, n_mem, CX_W), lambda bi, i: (bi, 0, 0)),
            pl.BlockSpec((CX_W, d), lambda bi, i: (0, 0)),
        ],
        out_specs=pl.BlockSpec((1, bm, d), lambda bi, i: (bi, i, 0)),
        out_shape=jax.ShapeDtypeStruct((b, l, d), F32),
        scratch_shapes=[pltpu.VMEM((bm, d), BF16)],
        compiler_params=_cparams(("parallel", "arbitrary")),
        name="cross_attention",
    )(x, g_x.reshape(1, d), wq, q_gain.reshape(1, HEAD_DIM), kmem, vmem, wo)


def _pack_bf16_pairs(x):
    half = x.shape[1] // 2
    bits = lax.bitcast_convert_type(x.astype(BF16).astype(F32), jnp.uint32)
    return (bits[:, :half] & jnp.uint32(0xFFFF0000)) | (bits[:, half:] >> 16)


def _unpack_bf16_pairs(w):
    hi = lax.bitcast_convert_type(w & jnp.uint32(0xFFFF0000), F32)
    lo = lax.bitcast_convert_type(w << 16, F32)
    return hi, lo


def _router_kernel(x_ref, g_ref, r_ref, idx_ref, w_ref, hp_ref, *, n_experts):
    hn = _rms_rows(x_ref[...], g_ref[...])
    hp_ref[...] = _pack_bf16_pairs(hn)
    hi = hn.astype(BF16)
    lo = (hn - hi.astype(F32)).astype(BF16)
    hh = jnp.dot(hi, r_ref[...], preferred_element_type=F32)
    logits = (hh[:, :LANES] + hh[:, LANES:]
              + jnp.dot(lo, r_ref[:, :LANES], preferred_element_type=F32))
    lane = lax.broadcasted_iota(jnp.int32, logits.shape, 1).astype(F32)
    low = jnp.float32(-3.0e38)
    l1 = jnp.where(lane < n_experts, logits, low)
    m1 = jnp.max(l1, axis=-1, keepdims=True)
    i1 = jnp.min(jnp.where(l1 == m1, lane, float(LANES)), axis=-1, keepdims=True)
    l2 = jnp.where(lane == i1, low, l1)
    m2 = jnp.max(l2, axis=-1, keepdims=True)
    i2 = jnp.min(jnp.where(l2 == m2, lane, float(LANES)), axis=-1, keepdims=True)
    e2 = jnp.exp(m2 - m1)
    w1 = 1.0 / (1.0 + e2)
    w2 = e2 / (1.0 + e2)
    idx_ref[...] = jnp.where(lane == 0.0, i1, jnp.where(lane == 1.0, i2, 0.0)).astype(jnp.int32)
    w_ref[...] = jnp.where(lane == 0.0, w1, jnp.where(lane == 1.0, w2, 0.0))


def route_tokens(x, gain, router):
    t, d = x.shape
    n_experts = router.shape[1]
    r = jnp.zeros((d, LANES), F32).at[:, :n_experts].set(router.astype(F32))
    r_hi = r.astype(BF16)
    r_lo = (r - r_hi.astype(F32)).astype(BF16)
    r_split = jnp.concatenate([r_hi, r_lo], axis=1)
    bm = _pick(t, (256, 128, 64, 32, 16, 8))
    return pl.pallas_call(
        functools.partial(_router_kernel, n_experts=n_experts),
        grid=(t // bm,),
        in_specs=[
            pl.BlockSpec((bm, d), lambda i: (i, 0)),
            pl.BlockSpec((1, d), lambda i: (0, 0)),
            pl.BlockSpec((d, 2 * LANES), lambda i: (0, 0)),
        ],
        out_specs=[pl.BlockSpec((bm, LANES), lambda i: (i, 0)), pl.BlockSpec((bm, LANES), lambda i: (i, 0)),
                   pl.BlockSpec((bm, d // 2), lambda i: (i, 0))],
        out_shape=[jax.ShapeDtypeStruct((t, LANES), jnp.int32), jax.ShapeDtypeStruct((t, LANES), F32),
                   jax.ShapeDtypeStruct((t, d // 2), jnp.uint32)],
        compiler_params=_cparams(("parallel",)),
        name="moe_router",
    )(x, gain.reshape(1, d), r_split)


def _swiglu_step(hn, wg_ref, wu_ref, wd_ref):
    a = jnp.dot(hn, wg_ref[0], preferred_element_type=F32)
    b = jnp.dot(hn, wu_ref[0], preferred_element_type=F32)
    hid = (a * jax.nn.sigmoid(a)) * b
    return jnp.dot(hid.astype(BF16), wd_ref[0], preferred_element_type=F32)


def _ffn_kernel(x_hbm, g_ref, wg_ref, wu_ref, wd_ref, o_ref, hn_ref, xbuf, sem, *, bm, n_blocks):
    i = pl.program_id(0)

    def fetch(blk):
        return pltpu.make_async_copy(x_hbm.at[pl.ds(pl.multiple_of(blk * bm, bm), bm)], xbuf, sem)

    @pl.when((i == 0) & (pl.program_id(1) == 0))
    def _():
        fetch(0).start()

    @pl.when(pl.program_id(1) == 0)
    def _():
        fetch(i).wait()
        _norm_block_to(xbuf, g_ref, hn_ref)
        o_ref[...] = xbuf[...]

        @pl.when(i + 1 < n_blocks)
        def _():
            fetch(i + 1).start()

    o_ref[...] += _swiglu_step(hn_ref[...], wg_ref, wu_ref, wd_ref)


FFN_TILE = 512
FFN_VMEM_LIMIT_BYTES = 60 * 1024 * 1024


def swiglu_ffn(x, gain, w_gu, w_down):
    t, d = x.shape
    ffn = w_down.shape[1]
    bm = _pick(t, (512, 256, 128, 64, 32, 16, 8))
    tf = _pick(ffn, (FFN_TILE, 256, 128))
    nf = ffn // tf
    n_blocks = t // bm
    return pl.pallas_call(
        functools.partial(_ffn_kernel, bm=bm, n_blocks=n_blocks),
        grid=(n_blocks, nf),
        in_specs=[
            pl.BlockSpec(memory_space=pl.ANY),
            pl.BlockSpec((1, d), lambda i, f: (0, 0)),
            pl.BlockSpec((1, d, tf), lambda i, f: (0, 0, f)),
            pl.BlockSpec((1, d, tf), lambda i, f: (0, 0, nf + f)),
            pl.BlockSpec((1, tf, d), lambda i, f: (0, f, 0)),
        ],
        out_specs=pl.BlockSpec((bm, d), lambda i, f: (i, 0)),
        out_shape=jax.ShapeDtypeStruct((t, d), F32),
        scratch_shapes=[pltpu.VMEM((bm, d), BF16), pltpu.VMEM((bm, d), F32), pltpu.SemaphoreType.DMA(())],
        compiler_params=_cparams(("arbitrary", "arbitrary"), FFN_VMEM_LIMIT_BYTES),
        name="dense_ffn",
    )(x, gain.reshape(1, d), w_gu, w_gu, w_down)


def _pad_ffn_weights(w_gu, w_down, tile):
    ffn = w_down.shape[0]
    pad = (-ffn) % tile
    wg = jnp.pad(w_gu[:, :ffn], ((0, 0), (0, pad)))
    wu = jnp.pad(w_gu[:, ffn:], ((0, 0), (0, pad)))
    return (jnp.concatenate([wg, wu], axis=1).astype(BF16),
            jnp.pad(w_down, ((0, pad), (0, 0))).astype(BF16))


MOE_TILE_ROWS = 512
MOE_MOVE_TOKENS = 256
MOE_ZERO_ROWS = 128
MOE_ISSUE_UNROLL = 8


def _moe_plan(top_idx, n_e, bm, n_tiles):
    e_flat = top_idx[:, :TOP_K].reshape(-1)
    onehot = (e_flat[:, None] == jnp.arange(n_e, dtype=jnp.int32)[None, :]).astype(jnp.int32)
    csum = jnp.cumsum(onehot, axis=0)
    cnt = csum[-1]
    rank = jnp.sum(onehot * (csum - 1), axis=1)
    padded = ((cnt + bm - 1) // bm) * bm
    gend = jnp.cumsum(padded)
    gstart = gend - padded
    dest = (jnp.sum(onehot * gstart[None, :], axis=1) + rank).astype(jnp.int32)
    n_valid = (gend[-1] // bm).astype(jnp.int32)
    tile_start = jnp.arange(n_tiles, dtype=jnp.int32) * bm
    tile_e = jnp.sum((tile_start[:, None] >= gend[None, :]).astype(jnp.int32), axis=1)
    tile_e = jnp.minimum(tile_e, n_e - 1)
    last_e = jnp.sum(jnp.where(jnp.arange(n_tiles) == n_valid - 1, tile_e, 0))
    tile_e = jnp.where(jnp.arange(n_tiles) < n_valid, tile_e, last_e).astype(jnp.int32)
    return dest, tile_e, n_valid.reshape(1), gend.astype(jnp.int32), cnt.astype(jnp.int32)


def _dispatch_kernel(gend_ref, cnt_ref, nv_ref, dest_ref, x_ref, xs_hbm, zbuf, sem, zsem, *,
                     bt, bm, n_e, n_tiles):
    i = pl.program_id(0)

    @pl.when(i == 0)
    def _():
        zbuf[...] = jnp.zeros_like(zbuf)
        zr = zbuf.shape[0]

        def clear_tile(base):
            copies = [pltpu.make_async_copy(
                zbuf, xs_hbm.at[pl.ds(pl.multiple_of(base + z * zr, zr), zr)], zsem)
                for z in range(bm // zr)]
            for cp in copies:
                cp.start()
            for cp in copies:
                cp.wait()

        for e in range(n_e):
            @pl.when(cnt_ref[e] > 0)
            def _():
                clear_tile(gend_ref[e] - bm)

        for back in range(1, min(n_e, n_tiles) + 1):
            @pl.when(n_tiles - back >= nv_ref[0])
            def _():
                clear_tile((n_tiles - back) * bm)

    def issue(r, carry):
        for k in range(TOP_K):
            pltpu.make_async_copy(x_ref.at[pl.ds(r, 1)],
                                  xs_hbm.at[pl.ds(dest_ref[0, 0, TOP_K * r + k], 1)], sem).start()
        return carry

    lax.fori_loop(0, bt, issue, 0, unroll=MOE_ISSUE_UNROLL)
    for k in range(TOP_K):
        pltpu.make_async_copy(x_ref, xs_hbm.at[pl.ds(0, bt)], sem).wait()


def moe_dispatch(x, dest, gend, cnt, n_valid, n_tiles, bm):
    t, d = x.shape
    n_rows = n_tiles * bm
    bt = _pick(t, (MOE_MOVE_TOKENS, 64, 32, 16, 8))
    n_steps = t // bt
    zr = min(MOE_ZERO_ROWS, bm)
    return pl.pallas_call(
        functools.partial(_dispatch_kernel, bt=bt, bm=bm, n_e=cnt.shape[0], n_tiles=n_tiles),
        grid_spec=pltpu.PrefetchScalarGridSpec(
            num_scalar_prefetch=3,
            grid=(n_steps,),
            in_specs=[
                pl.BlockSpec((1, 1, TOP_K * bt), lambda i, ge, cn, nv: (i, 0, 0), memory_space=pltpu.SMEM),
                pl.BlockSpec((bt, d), lambda i, ge, cn, nv: (i, 0)),
            ],
            out_specs=pl.BlockSpec(memory_space=pl.ANY),
            scratch_shapes=[pltpu.VMEM((zr, d), x.dtype), pltpu.SemaphoreType.DMA(()),
                            pltpu.SemaphoreType.DMA(())],
        ),
        out_shape=jax.ShapeDtypeStruct((n_rows, d), x.dtype),
        compiler_params=_cparams(("arbitrary",)),
        name="moe_dispatch",
    )(gend, cnt, n_valid, dest.reshape(n_steps, 1, TOP_K * bt), x)


def _moe_ffn_kernel(te_ref, nv_ref, x_ref, wg_ref, wu_ref, wd_ref, o_ref, hn_ref, acc_ref, *, nf, slab):
    j = pl.program_id(0)
    f = pl.program_id(1)
    valid = j < nv_ref[0]
    rows, half = x_ref.shape

    @pl.when(valid & (f == 0))
    def _():
        def unpack(r, carry):
            r0 = pl.multiple_of(r * slab, slab)
            hi, lo = _unpack_bf16_pairs(x_ref[pl.ds(r0, slab), :])
            hn_ref[pl.ds(r0, slab), pl.ds(0, half)] = hi.astype(hn_ref.dtype)
            hn_ref[pl.ds(r0, slab), pl.ds(half, half)] = lo.astype(hn_ref.dtype)
            return carry

        lax.fori_loop(0, rows // slab, unpack, 0)
        acc_ref[...] = _swiglu_step(hn_ref[...], wg_ref, wu_ref, wd_ref)

    @pl.when(valid & (f > 0))
    def _():
        acc_ref[...] += _swiglu_step(hn_ref[...], wg_ref, wu_ref, wd_ref)

    @pl.when(valid & (f == nf - 1))
    def _():
        def pack(r, carry):
            r0 = pl.multiple_of(r * slab, slab)
            o_ref[pl.ds(r0, slab), :] = _pack_bf16_pairs(acc_ref[pl.ds(r0, slab), :])
            return carry

        lax.fori_loop(0, rows // slab, pack, 0)

    @pl.when(jnp.logical_not(valid) & (f == nf - 1))
    def _():
        o_ref[...] = jnp.zeros_like(o_ref)


def moe_expert_ffn(xs, w_gu, w_down, tile_e, n_valid, bm):
    n_rows, half = xs.shape
    d = 2 * half
    n_tiles = n_rows // bm
    slab = min(64, bm)
    ffn = w_down.shape[1]
    tf = _pick(ffn, (256, 128))
    nf = ffn // tf

    def fidx(j, f, nv):
        return jnp.where(j < nv[0], f, nf - 1)

    return pl.pallas_call(
        functools.partial(_moe_ffn_kernel, nf=nf, slab=slab),
        grid_spec=pltpu.PrefetchScalarGridSpec(
            num_scalar_prefetch=2,
            grid=(n_tiles, nf),
            in_specs=[
                pl.BlockSpec((bm, half), lambda j, f, te, nv: (jnp.minimum(j, nv[0] - 1), 0)),
                pl.BlockSpec((1, d, tf), lambda j, f, te, nv: (te[j], 0, fidx(j, f, nv))),
                pl.BlockSpec((1, d, tf), lambda j, f, te, nv: (te[j], 0, nf + fidx(j, f, nv))),
                pl.BlockSpec((1, tf, d), lambda j, f, te, nv: (te[j], fidx(j, f, nv), 0)),
            ],
            out_specs=pl.BlockSpec((bm, half), lambda j, f, te, nv: (j, 0)),
            scratch_shapes=[pltpu.VMEM((bm, d), BF16), pltpu.VMEM((bm, d), F32)],
        ),
        out_shape=jax.ShapeDtypeStruct((n_rows, half), jnp.uint32),
        compiler_params=_cparams(("arbitrary", "arbitrary")),
        name="moe_ffn",
    )(tile_e, n_valid, xs, w_gu, w_gu, w_down)


def _combine_kernel(pos_ref, posn_ref, x_ref, w_ref, y_hbm, o_ref, ybuf, sems, *, bt, n_steps):
    i = pl.program_id(0)
    slot = i % 2

    def issue(p_ref, s):
        def body(r, carry):
            for k in range(TOP_K):
                pltpu.make_async_copy(y_hbm.at[pl.ds(p_ref[0, 0, TOP_K * r + k], 1)],
                                      ybuf.at[s, k, pl.ds(r, 1)], sems.at[s]).start()
            return carry

        lax.fori_loop(0, bt, body, 0, unroll=MOE_ISSUE_UNROLL)

    @pl.when(i == 0)
    def _():
        issue(pos_ref, 0)

    @pl.when(i + 1 < n_steps)
    def _():
        issue(posn_ref, 1 - slot)

    for k in range(TOP_K):
        pltpu.make_async_copy(y_hbm.at[pl.ds(0, bt)], ybuf.at[slot, k], sems.at[slot]).wait()
    w = w_ref[...]
    half = ybuf.shape[-1]
    a_hi, a_lo = _unpack_bf16_pairs(ybuf[slot, 0])
    b_hi, b_lo = _unpack_bf16_pairs(ybuf[slot, 1])
    o_ref[:, :half] = x_ref[:, :half] + w[:, 0:1] * a_hi + w[:, 1:2] * b_hi
    o_ref[:, half:] = x_ref[:, half:] + w[:, 0:1] * a_lo + w[:, 1:2] * b_lo


def moe_combine(x, top_w, dest, ys):
    t, d = x.shape
    bt = _pick(t, (MOE_MOVE_TOKENS, 64, 32, 16, 8))
    n_steps = t // bt
    pos = dest.reshape(n_steps, 1, TOP_K * bt)
    return pl.pallas_call(
        functools.partial(_combine_kernel, bt=bt, n_steps=n_steps),
        grid=(n_steps,),
        in_specs=[
            pl.BlockSpec((1, 1, TOP_K * bt), lambda i: (i, 0, 0), memory_space=pltpu.SMEM),
            pl.BlockSpec((1, 1, TOP_K * bt), lambda i: (jnp.minimum(i + 1, n_steps - 1), 0, 0),
                         memory_space=pltpu.SMEM),
            pl.BlockSpec((bt, d), lambda i: (i, 0)),
            pl.BlockSpec((bt, LANES), lambda i: (i, 0)),
            pl.BlockSpec(memory_space=pl.ANY),
        ],
        out_specs=pl.BlockSpec((bt, d), lambda i: (i, 0)),
        out_shape=jax.ShapeDtypeStruct((t, d), F32),
        scratch_shapes=[pltpu.VMEM((2, TOP_K, bt, d // 2), jnp.uint32), pltpu.SemaphoreType.DMA((2,))],
        compiler_params=_cparams(("arbitrary",)),
        name="moe_combine",
    )(pos, pos, x, top_w, ys)


def moe_ffn(x, gain, router, w_gu, w_down):
    t, d = x.shape
    n_e = w_down.shape[0]
    bm = min(MOE_TILE_ROWS, max(8, t // 8))
    n_tiles = (TOP_K * t + n_e * (bm - 1) + bm - 1) // bm
    top_idx, top_w, hn_packed = route_tokens(x, gain, router)
    dest, tile_e, n_valid, gend, cnt = _moe_plan(top_idx, n_e, bm, n_tiles)
    xs = moe_dispatch(hn_packed, dest, gend, cnt, n_valid, n_tiles, bm)
    ys = moe_expert_ffn(xs, w_gu, w_down, tile_e, n_valid, bm)
    return moe_combine(x, top_w, dest, ys)


def _prep_layer(i, p):
    w_in = p["w_in"][i]
    d = w_in.shape[0]
    w_main = jnp.concatenate([w_in[:, :OFF_GATE_END], w_in[:, OFF_DECAY_END:]], axis=1).astype(BF16)
    w_decay = jnp.zeros((d, LANES), BF16).at[:, :2 * GLA_RANK].set(
        w_in[:, OFF_GATE_END:OFF_DECAY_END].astype(BF16))

    def up_mat(up, row0):
        u = up.reshape(GLA_RANK, GLA_HEADS, GLA_DK).transpose(1, 0, 2).astype(BF16)
        return jnp.zeros((GLA_HEADS, LANES, GLA_DK), BF16).at[:, row0:row0 + GLA_RANK, :].set(u)

    return dict(
        norm_mix=p["norm_mix"][i], w_main=w_main, w_decay=w_decay,
        u_f=up_mat(p["gla_up_f"][i], 0), u_b=up_mat(p["gla_up_b"][i], GLA_RANK),
        bias_f=p["gla_bias_f"][i].reshape(GLA_HEADS, 1, GLA_DK).astype(F32),
        bias_b=p["gla_bias_b"][i].reshape(GLA_HEADS, 1, GLA_DK).astype(F32),
        gla_out_norm=p["gla_out_norm"][i], swa_q_norm=p["swa_q_norm"][i], swa_k_norm=p["swa_k_norm"][i],
        swa_sink=p["swa_sink"][i], w_out=p["w_out"][i].astype(BF16),
        norm_cross=p["norm_cross"][i], norm_mem=p["norm_mem"][i],
        cx_wq=p["cx_wq"][i].astype(BF16), cx_wkv=p["cx_wkv"][i].astype(BF16),
        cx_q_norm=p["cx_q_norm"][i], cx_k_norm=p["cx_k_norm"][i], cx_wo=p["cx_wo"][i].astype(BF16),
        norm_ffn=p["norm_ffn"][i],
    )


def _trunk(x, mem, layers, bias_tab):
    b, l, d = x.shape
    t = b * l
    for i, lw in enumerate(layers):
        proj, decay_in = in_proj(x.reshape(t, d), lw["norm_mix"], lw["w_main"], lw["w_decay"])
        proj = proj.reshape(b, l, MAIN_W)
        decay_in = decay_in.reshape(b, l, LANES)
        o_f = gla_direction(proj, decay_in, lw["u_f"], lw["bias_f"], reverse=False)
        o_gla = gla_direction(proj, decay_in, lw["u_b"], lw["bias_b"], reverse=True,
                              o_fwd=o_f, out_gain=lw["gla_out_norm"])
        o_swa = swa_mixer(proj, bias_tab, lw["swa_sink"], lw["swa_q_norm"], lw["swa_k_norm"])
        x1 = out_proj(o_gla.reshape(t, GLA_V_W), o_swa.reshape(t, SWA_Q_W), lw["w_out"], x.reshape(t, d))
        kmem, vmem = cross_kv(mem, lw["norm_mem"], lw["cx_wkv"], lw["cx_k_norm"])
        x2 = cross_attention(x1.reshape(b, l, d), kmem, vmem, lw["norm_cross"], lw["cx_wq"],
                             lw["cx_q_norm"], lw["cx_wo"]).reshape(t, d)
        if i % 2 == 0:
            x3 = swiglu_ffn(x2, lw["norm_ffn"], lw["ffn_w_gu"], lw["ffn_w_down"])
        else:
            x3 = moe_ffn(x2, lw["norm_ffn"], lw["moe_router"], lw["moe_w_gu"], lw["moe_w_down"])
        x = x3.reshape(b, l, d)
    return x


def kernel(x_prompt, x_sample, mem_prompt, mem_sample, rel_bias, norm_mix, w_in, gla_up_f, gla_bias_f, gla_up_b, gla_bias_b, gla_out_norm, swa_q_norm, swa_k_norm, swa_sink, w_out, norm_cross, norm_mem, cx_wq, cx_wkv, cx_q_norm, cx_k_norm, cx_wo, norm_ffn, ffn_w_gu, ffn_w_down, moe_router, moe_w_gu, moe_w_down):
    p = dict(norm_mix=norm_mix, w_in=w_in, gla_up_f=gla_up_f, gla_bias_f=gla_bias_f, gla_up_b=gla_up_b,
             gla_bias_b=gla_bias_b, gla_out_norm=gla_out_norm, swa_q_norm=swa_q_norm, swa_k_norm=swa_k_norm,
             swa_sink=swa_sink, w_out=w_out, norm_cross=norm_cross, norm_mem=norm_mem, cx_wq=cx_wq,
             cx_wkv=cx_wkv, cx_q_norm=cx_q_norm, cx_k_norm=cx_k_norm, cx_wo=cx_wo, norm_ffn=norm_ffn)
    depth = w_in.shape[0]
    layers = []
    for i in range(depth):
        lw = _prep_layer(i, p)
        if i % 2 == 0:
            w_gu_p, w_down_p = _pad_ffn_weights(ffn_w_gu[i // 2], ffn_w_down[i // 2], FFN_TILE)
            lw["ffn_w_gu"] = w_gu_p[None]
            lw["ffn_w_down"] = w_down_p[None]
        else:
            lw["moe_router"] = moe_router[i // 2]
            lw["moe_w_gu"] = moe_w_gu[i // 2].astype(BF16)
            lw["moe_w_down"] = moe_w_down[i // 2].astype(BF16)
        layers.append(lw)
    bias_tab = swa_bias_table(rel_bias)
    y_prompt = _trunk(x_prompt, mem_prompt, layers, bias_tab)
    y_sample = _trunk(x_sample, mem_sample, layers, bias_tab)
    return (y_prompt, y_sample)
```
